```python
import jax
import jax.numpy as jnp
from jax import lax
import numpy as np

D_MODEL = 2048
BATCH = 4
SEQ = 4096
DEPTH = 4

N_MEM = 256
NORM_EPS = 1e-6

RWKV_HEADS = 16
RWKV_HEAD_DIM = 64
RWKV_WIDTH = RWKV_HEADS * RWKV_HEAD_DIM
DECAY_LORA = 96
ICLR_LORA = 96
VRES_LORA = 64
GATE_LORA = 256
RWKV_GN_EPS = 64e-5

MLA_HEADS = 8
QK_NOPE_DIM = 128
QK_ROPE_DIM = 64
V_HEAD_DIM = 128
Q_LORA_RANK = 512
KV_LORA_RANK = 256
MLA_WIDTH = MLA_HEADS * V_HEAD_DIM
ROPE_THETA = 10000.0
Q_BLOCK = 128

MIX_WIDTH = RWKV_WIDTH + MLA_WIDTH
RWKV_IN = 3 * RWKV_WIDTH + DECAY_LORA + ICLR_LORA + GATE_LORA
MLA_IN = Q_LORA_RANK + KV_LORA_RANK + QK_ROPE_DIM
IN_WIDTH = RWKV_IN + MLA_IN

MEM_HEADS = 4
MEM_HEAD_DIM = 128
MEM_WIDTH = MEM_HEADS * MEM_HEAD_DIM

D_FF = 4 * D_MODEL

kernel_name = 'hybrid_rwkv7_mla_memory_block'


def rms_norm(x, g):
    x32 = x.astype(jnp.float32)
    y = x32 * lax.rsqrt(jnp.mean(jnp.square(x32), axis=-1, keepdims=True) + NORM_EPS)
    return (y * g.astype(jnp.float32)).astype(x.dtype)


def token_shift(y, mu):
    y_prev = jnp.pad(y[:, :-1], ((0, 0), (1, 0), (0, 0)))
    return y + (y_prev - y) * mu


def rope_tables(seq_len):
    pos = jnp.arange(seq_len, dtype=jnp.float32)
    inv_freq = ROPE_THETA ** (-jnp.arange(0, QK_ROPE_DIM, 2, dtype=jnp.float32) / QK_ROPE_DIM)
    ang = pos[:, None] * inv_freq[None, :]
    return jnp.cos(ang), jnp.sin(ang)


def apply_rope(x, cos, sin):
    half = x.shape[-1] // 2
    x1 = x[..., :half].astype(jnp.float32)
    x2 = x[..., half:].astype(jnp.float32)
    return jnp.concatenate([x1 * cos - x2 * sin, x1 * sin + x2 * cos], axis=-1).astype(x.dtype)


def rwkv7_scan(r, decay, k, v, kk, a):
    B, S, H, N = r.shape

    def step(state, inp):
        r_t, w_t, k_t, v_t, kk_t, b_t = inp
        sa = jnp.einsum('bhvk,bhk->bhv', state, -kk_t)
        state = (state * w_t[:, :, None, :]
                 + sa[..., :, None] * b_t[:, :, None, :]
                 + v_t[..., :, None] * k_t[:, :, None, :])
        return state, jnp.einsum('bhvk,bhk->bhv', state, r_t)

    xs = tuple(jnp.moveaxis(t, 1, 0) for t in (r, decay, k, v, kk, kk * a))
    _, out = lax.scan(step, jnp.zeros((B, H, N, N), jnp.float32), xs)
    return jnp.moveaxis(out, 0, 1)


def rwkv7_time_mix(cols, mu, w0, w_up, a0, a_up, g_up, k_k, k_a, r_k, lnx_g, lnx_b,
                   v_first, vres_cols, mu_vres, v0, v_up):
    B, S, _ = cols.shape
    H, N, C = RWKV_HEADS, RWKV_HEAD_DIM, RWKV_WIDTH
    f32 = jnp.float32
    c = token_shift(cols, mu)
    r, k, v, wl, al, gl = jnp.split(
        c, [C, 2 * C, 3 * C, 3 * C + DECAY_LORA, 3 * C + DECAY_LORA + ICLR_LORA], axis=-1)
    log_w = -jax.nn.softplus(-(w0 + jnp.tanh(wl) @ w_up).astype(f32)) - 0.5
    a = jax.nn.sigmoid((a0 + al @ a_up).astype(f32))
    g = jax.nn.sigmoid(gl) @ g_up
    if v_first is None:
        v_first = v
    else:
        vr = token_shift(vres_cols, mu_vres)
        v = v + (v_first - v) * jax.nn.sigmoid(v0 + vr @ v_up)

    def heads(t):
        return t.reshape(B, S, H, N).astype(f32)

    r, k, v, a, log_w = heads(r), heads(k), heads(v), heads(a), heads(log_w)
    kk = k * k_k.reshape(H, N).astype(f32)
    kk = kk * lax.rsqrt(jnp.maximum(jnp.sum(kk * kk, axis=-1, keepdims=True), 1e-24))
    k = k * (1.0 + (a - 1.0) * k_a.reshape(H, N).astype(f32))
    decay = jnp.exp(-jnp.exp(log_w))
    o = rwkv7_scan(r, decay, k, v, kk, a)
    mean = jnp.mean(o, axis=-1, keepdims=True)
    var = jnp.mean(jnp.square(o - mean), axis=-1, keepdims=True)
    o = ((o - mean) * lax.rsqrt(var + RWKV_GN_EPS)).reshape(B, S, C) * lnx_g + lnx_b
    bonus = jnp.sum(r * k * r_k.astype(f32), axis=-1, keepdims=True) * v
    o = o + bonus.reshape(B, S, C)
    return (o * g).astype(cols.dtype), v_first


def causal_block_attention(qn, qr, kn, kr, v):
    S = qn.shape[1]
    scale = (QK_NOPE_DIM + QK_ROPE_DIM) ** -0.5
    outs = []
    for i in range(S // Q_BLOCK):
        q0, q1 = i * Q_BLOCK, (i + 1) * Q_BLOCK
        s = (jnp.einsum('bqhd,bkhd->bhqk', qn[:, q0:q1], kn[:, :q1])
             + jnp.einsum('bqhr,bkr->bhqk', qr[:, q0:q1], kr[:, :q1])).astype(jnp.float32) * scale
        mask = (q0 + jnp.arange(Q_BLOCK))[:, None] >= jnp.arange(q1)[None, :]
        s = jnp.where(mask, s, jnp.finfo(jnp.float32).min)
        p = jax.nn.softmax(s, axis=-1).astype(v.dtype)
        outs.append(jnp.einsum('bhqk,bkhd->bqhd', p, v[:, :q1]))
    return jnp.concatenate(outs, axis=1)


def mla_heads(cols, q_norm_g, w_uq, kv_norm_g, w_ukv, cos, sin):
    B, S, _ = cols.shape
    cq, ckv, kr = jnp.split(cols, [Q_LORA_RANK, Q_LORA_RANK + KV_LORA_RANK], axis=-1)
    q = (rms_norm(cq, q_norm_g) @ w_uq).reshape(B, S, MLA_HEADS, QK_NOPE_DIM + QK_ROPE_DIM)
    kv = (rms_norm(ckv, kv_norm_g) @ w_ukv).reshape(B, S, MLA_HEADS, QK_NOPE_DIM + V_HEAD_DIM)
    qn, qr = q[..., :QK_NOPE_DIM], q[..., QK_NOPE_DIM:]
    kn, v = kv[..., :QK_NOPE_DIM], kv[..., QK_NOPE_DIM:]
    qr = apply_rope(qr, cos[:, None, :], sin[:, None, :])
    kr = apply_rope(kr, cos, sin)
    o = causal_block_attention(qn, qr, kn, kr, v)
    return o.reshape(B, S, MLA_WIDTH)


def memory_cross_attention(u, mem_n, wq, wk, wv, wo):
    B, S, _ = u.shape
    M = mem_n.shape[1]
    q = (u @ wq).reshape(B, S, MEM_HEADS, MEM_HEAD_DIM)
    k = (mem_n @ wk).reshape(B, M, MEM_HEADS, MEM_HEAD_DIM)
    v = (mem_n @ wv).reshape(B, M, MEM_HEADS, MEM_HEAD_DIM)
    s = jnp.einsum('bqhd,bmhd->bhqm', q, k).astype(jnp.float32) * MEM_HEAD_DIM ** -0.5
    p = jax.nn.softmax(s, axis=-1).astype(v.dtype)
    o = jnp.einsum('bhqm,bmhd->bqhd', p, v).reshape(B, S, MEM_WIDTH)
    return o @ wo


def setup_inputs(seed: int = 0) -> dict:
    key = jax.random.key(seed)
    keys = iter(jax.random.split(key, 48))
    f32 = jnp.float32

    def normal(shape, scale):
        return scale * jax.random.normal(next(keys), shape, f32)

    def gain(shape):
        return 1.0 + 0.05 * jax.random.normal(next(keys), shape, f32)

    def uniform(shape, lo, hi):
        return jax.random.uniform(next(keys), shape, f32, lo, hi)

    L, L1, D = DEPTH, DEPTH - 1, D_MODEL
    return {
        'x': normal((BATCH, SEQ, D), 1.0),
        'mem': normal((BATCH, N_MEM, D), 1.0),
        'mem_norm_g': gain((D,)),
        'mix_pre_g': gain((L, D)),
        'w_in': normal((L, D, IN_WIDTH), D ** -0.5),
        'w_in_vres': normal((L1, D, VRES_LORA), D ** -0.5),
        'mu_rwkv': uniform((L, RWKV_IN), 0.0, 1.0),
        'mu_vres': uniform((L1, VRES_LORA), 0.0, 1.0),
        'w0': uniform((L, RWKV_WIDTH), -6.0, -1.0),
        'w_up': normal((L, DECAY_LORA, RWKV_WIDTH), 0.5 * DECAY_LORA ** -0.5),
        'a0': normal((L, RWKV_WIDTH), 0.1),
        'a_up': normal((L, ICLR_LORA, RWKV_WIDTH), ICLR_LORA ** -0.5),
        'v0': 1.0 + normal((L1, RWKV_WIDTH), 0.1),
        'v_up': normal((L1, VRES_LORA, RWKV_WIDTH), VRES_LORA ** -0.5),
        'g_up': normal((L, GATE_LORA, RWKV_WIDTH), GATE_LORA ** -0.5),
        'k_k': 0.85 + normal((L, RWKV_WIDTH), 0.05),
        'k_a': gain((L, RWKV_WIDTH)),
        'r_k': normal((L, RWKV_HEADS, RWKV_HEAD_DIM), 0.1),
        'lnx_g': gain((L, RWKV_WIDTH)),
        'lnx_b': normal((L, RWKV_WIDTH), 0.01),
        'q_norm_g': gain((L, Q_LORA_RANK)),
        'w_uq': normal((L, Q_LORA_RANK, MLA_HEADS * (QK_NOPE_DIM + QK_ROPE_DIM)), Q_LORA_RANK ** -0.5),
        'kv_norm_g': gain((L, KV_LORA_RANK)),
        'w_ukv': normal((L, KV_LORA_RANK, MLA_HEADS * (QK_NOPE_DIM + V_HEAD_DIM)), KV_LORA_RANK ** -0.5),
        'w_out': normal((L, MIX_WIDTH, D), MIX_WIDTH ** -0.5),
        'mix_post_g': gain((L, D)),
        'mem_pre_g': gain((L, D)),
        'wq_mem': normal((L, D, MEM_WIDTH), D ** -0.5),
        'wk_mem': normal((L, D, MEM_WIDTH), D ** -0.5),
        'wv_mem': normal((L, D, MEM_WIDTH), D ** -0.5),
        'wo_mem': normal((L, MEM_WIDTH, D), MEM_WIDTH ** -0.5),
        'mem_post_g': gain((L, D)),
        'ffn_pre_g': gain((L, D)),
        'w_ff1': normal((L, D, D_FF), D ** -0.5),
        'w_ff2': normal((L, D_FF, D), D_FF ** -0.5),
        'ffn_post_g': gain((L, D)),
    }


def reference(x, mem, mem_norm_g, mix_pre_g, w_in, w_in_vres, mu_rwkv, mu_vres, w0, w_up, a0, a_up,
              v0, v_up, g_up, k_k, k_a, r_k, lnx_g, lnx_b, q_norm_g, w_uq, kv_norm_g, w_ukv, w_out,
              mix_post_g, mem_pre_g, wq_mem, wk_mem, wv_mem, wo_mem, mem_post_g, ffn_pre_g, w_ff1,
              w_ff2, ffn_post_g):
    cos, sin = rope_tables(x.shape[1])
    mem_n = rms_norm(mem, mem_norm_g)
    h = x
    v_first = None
    for l in range(DEPTH):
        u = rms_norm(h, mix_pre_g[l])
        if l == 0:
            proj = u @ w_in[0]
            vres_cols, mu_v, v0_l, v_up_l = None, None, None, None
        else:
            proj = u @ jnp.concatenate([w_in[l], w_in_vres[l - 1]], axis=1)
            vres_cols, mu_v, v0_l, v_up_l = proj[..., IN_WIDTH:], mu_vres[l - 1], v0[l - 1], v_up[l - 1]
        y_rwkv, v_first = rwkv7_time_mix(
            proj[..., :RWKV_IN], mu_rwkv[l], w0[l], w_up[l], a0[l], a_up[l], g_up[l], k_k[l], k_a[l],
            r_k[l], lnx_g[l], lnx_b[l], v_first, vres_cols, mu_v, v0_l, v_up_l)
        y_mla = mla_heads(proj[..., RWKV_IN:IN_WIDTH], q_norm_g[l], w_uq[l], kv_norm_g[l], w_ukv[l], cos, sin)
        y = jnp.concatenate([y_rwkv, y_mla], axis=-1) @ w_out[l]
        h = h + rms_norm(y, mix_post_g[l])
        u = rms_norm(h, mem_pre_g[l])
        y = memory_cross_attention(u, mem_n, wq_mem[l], wk_mem[l], wv_mem[l], wo_mem[l])
        h = h + rms_norm(y, mem_post_g[l])
        u = rms_norm(h, ffn_pre_g[l])
        y = jnp.square(jax.nn.relu(u @ w_ff1[l])) @ w_ff2[l]
        h = h + rms_norm(y, ffn_post_g[l])
    return h
```

```python
import functools

import jax
import jax.numpy as jnp
from jax import lax
from jax.experimental import pallas as pl
from jax.experimental.pallas import tpu as pltpu

F32 = jnp.float32
BF16 = jnp.bfloat16

NORM_EPS = 1e-6
RWKV_GN_EPS = 64e-5
ROPE_THETA = 10000.0

RWKV_HEAD_DIM = 64
RWKV_WIDTH = 1024
DECAY_LORA = 96
ICLR_LORA = 96
GATE_LORA = 256
VRES_LORA = 64

MLA_HEADS = 8
QK_NOPE_DIM = 128
QK_ROPE_DIM = 64
V_HEAD_DIM = 128
Q_LORA_RANK = 512
KV_LORA_RANK = 256
MLA_WIDTH = MLA_HEADS * V_HEAD_DIM
Q_SLOT = 256

MEM_HEADS = 4
MEM_HEAD_DIM = 128

LANE = 128
SCAN_CHUNK = 64
SCAN_HEADS = 4
SCAN_TILE = SCAN_HEADS * RWKV_HEAD_DIM

OFF_R, OFF_K, OFF_V = 0, 1024, 2048
OFF_WL, OFF_AL, OFF_GL = 3072, 3200, 3328
RWKV_COLS = 3584
OFF_CQ, OFF_CKV, OFF_KR, OFF_VRES = 3584, 4096, 4352, 4480
PROJ_COLS = 4608

VMEM_LIMIT = 56 * 1024 * 1024


def _cparams(sem):
    return pltpu.CompilerParams(dimension_semantics=sem, vmem_limit_bytes=VMEM_LIMIT)


def _rms(x, g):
    ms = jnp.mean(x * x, axis=-1, keepdims=True)
    return x * lax.rsqrt(ms + NORM_EPS) * g


def _dot(a, b):
    return jnp.dot(a, b, preferred_element_type=F32)


def _dot_nt(a, b):
    return lax.dot_general(a, b, (((1,), (1,)), ((), ())), preferred_element_type=F32)


def _dot_tn(a, b):
    return lax.dot_general(a, b, (((0,), (0,)), ((), ())), preferred_element_type=F32)


def _split_dot(x, w):
    hi = x.astype(BF16)
    lo = (x - hi.astype(F32)).astype(BF16)
    return _dot(hi, w) + _dot(lo, w)


def _sigmoid(x):
    return 1.0 / (1.0 + jnp.exp(-x))


def _softplus(x):
    return jnp.maximum(x, 0.0) + jnp.log(1.0 + jnp.exp(-jnp.abs(x)))


def _group_ones(n, group):
    r = lax.broadcasted_iota(jnp.int32, (n, n), 0) // group
    c = lax.broadcasted_iota(jnp.int32, (n, n), 1) // group
    return (r == c).astype(BF16)


def _head_sum(x, ones):
    w = ones.shape[0]
    return jnp.concatenate(
        [_split_dot(x[:, c * w:(c + 1) * w], ones) for c in range(x.shape[1] // w)], axis=1)


def _norm_matmul_kernel(h_ref, g_ref, w_ref, o_ref, u_ref):
    @pl.when(pl.program_id(1) == 0)
    def _():
        u_ref[...] = _rms(h_ref[...], g_ref[...]).astype(BF16)

    o_ref[...] = _dot(u_ref[...], w_ref[...])


def _norm_matmul(h, g, w, *, tm, tn):
    t, d = h.shape
    n = w.shape[1]
    return pl.pallas_call(
        _norm_matmul_kernel,
        out_shape=jax.ShapeDtypeStruct((t, n), F32),
        grid=(t // tm, n // tn),
        in_specs=[pl.BlockSpec((tm, d), lambda i, j: (i, 0)),
                  pl.BlockSpec((1, d), lambda i, j: (0, 0)),
                  pl.BlockSpec((d, tn), lambda i, j: (0, j))],
        out_specs=pl.BlockSpec((tm, tn), lambda i, j: (i, j)),
        scratch_shapes=[pltpu.VMEM((tm, d), BF16)],
        compiler_params=_cparams(("parallel", "arbitrary")),
        name="norm_matmul",
    )(h, g.reshape(1, d), w)


def _shift_mix(y, prev8, mu, first):
    prev_row = jnp.where(first, 0.0, prev8[7:8, :])
    y_prev = pltpu.roll(y, 1, 0)
    row = lax.broadcasted_iota(jnp.int32, y.shape, 0)
    y_prev = jnp.where(row == 0, prev_row, y_prev)
    return y + (y_prev - y) * mu


def _rwkv_prep_kernel(*refs, seq_blocks, has_vres):
    if has_vres:
        (y_ref, yp_ref, mu_ref, w0_ref, wup_ref, a0_ref, aup_ref, gup_ref, kk_ref, ka_ref,
         yv_ref, yvp_ref, muv_ref, v0_ref, vup_ref, vf_ref,
         r_o, k_o, v_o, kkn_o, b_o, ld_o, g_o) = refs
    else:
        (y_ref, yp_ref, mu_ref, w0_ref, wup_ref, a0_ref, aup_ref, gup_ref, kk_ref, ka_ref,
         r_o, k_o, v_o, kkn_o, b_o, ld_o, g_o) = refs
    first = (pl.program_id(0) % seq_blocks) == 0
    c = _shift_mix(y_ref[...], yp_ref[...], mu_ref[...], first)
    r = c[:, OFF_R:OFF_R + RWKV_WIDTH]
    k = c[:, OFF_K:OFF_K + RWKV_WIDTH]
    v = c[:, OFF_V:OFF_V + RWKV_WIDTH]
    wl = c[:, OFF_WL:OFF_WL + LANE]
    al = c[:, OFF_AL:OFF_AL + LANE]
    gl = c[:, OFF_GL:OFF_GL + GATE_LORA]

    z = w0_ref[...] + _dot(jnp.tanh(wl).astype(BF16), wup_ref[...])
    log_w = -_softplus(-z) - 0.5
    ld_o[...] = -jnp.exp(log_w)
    a = _sigmoid(a0_ref[...] + _dot(al.astype(BF16), aup_ref[...]))
    g_o[...] = _dot(_sigmoid(gl).astype(BF16), gup_ref[...])
    if has_vres:
        vr = _shift_mix(yv_ref[...], yvp_ref[...], muv_ref[...], first)
        mix = _sigmoid(v0_ref[...] + _dot(vr.astype(BF16), vup_ref[...]))
        v = v + (vf_ref[...] - v) * mix
    ones = _group_ones(LANE, RWKV_HEAD_DIM)
    kk = k * kk_ref[...]
    kk = kk * lax.rsqrt(jnp.maximum(_head_sum(kk * kk, ones), 1e-24))
    r_o[...] = r
    k_o[...] = k * (1.0 + (a - 1.0) * ka_ref[...])
    v_o[...] = v
    kkn_o[...] = kk
    b_o[...] = kk * a


def _rwkv_prep(proj, lw, v_first, *, seq, tm):
    t = proj.shape[0]
    has_vres = v_first is not None
    cw = RWKV_WIDTH
    row = lambda i: (i, 0)
    prev = lambda i: (jnp.maximum(i * (tm // 8) - 1, 0), 0)
    const = lambda i: (0, 0)
    vec = pl.BlockSpec((1, cw), const)
    in_specs = [pl.BlockSpec((tm, RWKV_COLS), row), pl.BlockSpec((8, RWKV_COLS), prev),
                pl.BlockSpec((1, RWKV_COLS), const), vec, pl.BlockSpec((LANE, cw), const),
                vec, pl.BlockSpec((LANE, cw), const), pl.BlockSpec((GATE_LORA, cw), const), vec, vec]
    args = [proj, proj, lw["mu"], lw["w0"], lw["w_up"], lw["a0"], lw["a_up"], lw["g_up"],
            lw["k_k"], lw["k_a"]]
    if has_vres:
        vres_blk = OFF_VRES // LANE
        in_specs += [pl.BlockSpec((tm, LANE), lambda i: (i, vres_blk)),
                     pl.BlockSpec((8, LANE), lambda i: (jnp.maximum(i * (tm // 8) - 1, 0), vres_blk)),
                     pl.BlockSpec((1, LANE), const), vec, pl.BlockSpec((LANE, cw), const),
                     pl.BlockSpec((tm, cw), row)]
        args += [proj, proj, lw["mu_vres"], lw["v0"], lw["v_up"], v_first]
    out = jax.ShapeDtypeStruct((t, cw), F32)
    return pl.pallas_call(
        functools.partial(_rwkv_prep_kernel, seq_blocks=seq // tm, has_vres=has_vres),
        out_shape=[out] * 7,
        grid=(t // tm,),
        in_specs=in_specs,
        out_specs=[pl.BlockSpec((tm, cw), row)] * 7,
        compiler_params=_cparams(("parallel",)),
        name="rwkv_prep",
    )(*args)


def _rwkv_scan_kernel(r_ref, k_ref, v_ref, kk_ref, b_ref, ld_ref, g_ref, rk_ref, lg_ref, lb_ref,
                      y_ref, ht_ref, *, nbatch):
    cs, n = SCAN_CHUNK, SCAN_TILE

    @pl.when(pl.program_id(1) == 0)
    def _():
        ht_ref[...] = jnp.zeros_like(ht_ref)

    ri = lax.broadcasted_iota(jnp.int32, (n, n), 0)
    ci = lax.broadcasted_iota(jnp.int32, (n, n), 1)
    same_head = (ri // RWKV_HEAD_DIM) == (ci // RWKV_HEAD_DIM)
    strict = (ri % cs) > (ci % cs)
    incl = (ri % cs) >= (ci % cs)
    eye = (ri == ci).astype(F32)
    ones_bd = same_head.astype(BF16)
    tr = lax.broadcasted_iota(jnp.int32, (cs, cs), 0)
    tc = lax.broadcasted_iota(jnp.int32, (cs, cs), 1)
    tri = (tr >= tc).astype(BF16)
    sub_masks = []
    s = 1
    while s < cs:
        sub_masks.append(((ri // (2 * s)) == (ci // (2 * s))) & ((ri % (2 * s)) >= s) & ((ci % (2 * s)) < s))
        s *= 2

    def bd(x):
        return jnp.where(same_head, jnp.concatenate([x] * SCAN_HEADS, axis=0), 0.0).astype(BF16)

    for bi in range(nbatch):
        r, k, v = r_ref[bi], k_ref[bi], v_ref[bi]
        kk, b, ld = kk_ref[bi], b_ref[bi], ld_ref[bi]
        ld_hi = ld.astype(BF16)
        cum = _dot(tri, ld_hi) + _dot(tri, (ld - ld_hi.astype(F32)).astype(BF16))
        cum_c = cum[cs - 1:cs, :]
        gam = jnp.exp(cum)
        ginv = jnp.exp(-cum)
        gtail = jnp.exp(cum_c - cum)
        a_bd = bd(-kk * jnp.exp(cum - ld))
        r_bd = bd(r * gam)
        b_bd = bd(b * ginv)
        k_bd = bd(k * ginv)
        bp_bd = bd(b * gtail)
        kp_bd = bd(k * gtail)
        v_bd = bd(v)

        sc = _dot_nt(jnp.concatenate([a_bd, r_bd], axis=0), jnp.concatenate([b_bd, k_bd], axis=0))
        l_ab = sc[:n, :n]
        l_ak = jnp.where(strict, sc[:n, n:], 0.0).astype(BF16)
        m_rb = jnp.where(incl, sc[n:, :n], 0.0).astype(BF16)
        m_rk = jnp.where(incl, sc[n:, n:], 0.0).astype(BF16)

        t_inv = eye + jnp.where(sub_masks[0], l_ab, 0.0)
        for sm in sub_masks[1:]:
            x = _dot(jnp.where(sm, l_ab, 0.0).astype(BF16), t_inv.astype(BF16))
            t_inv = t_inv + _dot(t_inv.astype(BF16), x.astype(BF16))

        lv = _dot(l_ak, v_bd).astype(BF16)
        tx = _dot(t_inv.astype(BF16), jnp.concatenate([a_bd, lv], axis=1))
        ht = ht_ref[bi]
        ht_b = ht.astype(BF16)
        u = (_dot_nt(tx[:, :n].astype(BF16), ht_b) + tx[:, n:]).astype(BF16)
        o_bd = _dot_nt(r_bd, ht_b) + _dot(m_rb, u) + _dot(m_rk, v_bd)
        ht_ref[bi] = ht * jnp.exp(cum_c) + _dot_tn(u, bp_bd) + _dot_tn(v_bd, kp_bd)

        o = o_bd[0:cs] + o_bd[cs:2 * cs] + o_bd[2 * cs:3 * cs] + o_bd[3 * cs:4 * cs]
        inv_n = 1.0 / RWKV_HEAD_DIM
        mean = _split_dot(o, ones_bd) * inv_n
        d = o - mean
        var = _split_dot(d * d, ones_bd) * inv_n
        on = d * lax.rsqrt(var + RWKV_GN_EPS) * lg_ref[...] + lb_ref[...]
        bonus = _split_dot(r * k * rk_ref[...], ones_bd) * v
        y_ref[bi] = ((on + bonus) * g_ref[bi]).astype(y_ref.dtype)


def _rwkv_scan(r, k, v, kk, b, ld, g, r_k, lnx_g, lnx_b):
    bsz, seq, cw = r.shape
    blk = pl.BlockSpec((bsz, SCAN_CHUNK, SCAN_TILE), lambda h, c: (0, c, h))
    vec = pl.BlockSpec((1, SCAN_TILE), lambda h, c: (0, h))
    return pl.pallas_call(
        functools.partial(_rwkv_scan_kernel, nbatch=bsz),
        out_shape=jax.ShapeDtypeStruct((bsz, seq, cw), BF16),
        grid=(cw // SCAN_TILE, seq // SCAN_CHUNK),
        in_specs=[blk] * 7 + [vec] * 3,
        out_specs=blk,
        scratch_shapes=[pltpu.VMEM((bsz, SCAN_TILE, SCAN_TILE), F32)],
        compiler_params=_cparams(("parallel", "arbitrary")),
        name="rwkv_scan",
    )(r, k, v, kk, b, ld, g, r_k, lnx_g, lnx_b)


def _mla_prep_kernel(cq_ref, ckv_ref, kr_ref, gq_ref, gkv_ref, wa_ref, wb_ref, wkn_ref, wv_ref,
                     ct_ref, st_ref, kt_ref, q_o, kn_o, kr_o, v_o, *, scale):
    cqn = _rms(cq_ref[...], gq_ref[...]).astype(BF16)
    qa = _dot(cqn, wa_ref[...])
    qb = _dot(cqn, wb_ref[...])
    ct = ct_ref[...] * scale
    st = st_ref[...] * scale
    for h in range(MLA_HEADS):
        sl = slice(h * Q_SLOT, (h + 1) * Q_SLOT)
        q_o[:, sl] = (qa[:, sl] * ct + qb[:, sl] * st).astype(BF16)
    ckvn = _rms(ckv_ref[...], gkv_ref[...]).astype(BF16)
    kn_o[...] = _dot(ckvn, wkn_ref[...]).astype(BF16)
    v_o[...] = _dot(ckvn, wv_ref[...]).astype(BF16)
    kr = kr_ref[...] * kt_ref[...]
    kr = kr + pltpu.roll(kr, QK_ROPE_DIM, 1)
    lane = lax.broadcasted_iota(jnp.int32, kr.shape, 1)
    kr_o[...] = jnp.where(lane < QK_ROPE_DIM, kr, 0.0).astype(BF16)


def _mla_prep(proj, lw, tabs, *, seq, tm):
    t = proj.shape[0]
    sb = seq // tm
    const = lambda i: (0, 0)
    pos = lambda i: (i % sb, 0)
    qw = MLA_HEADS * Q_SLOT
    return pl.pallas_call(
        functools.partial(_mla_prep_kernel, scale=(QK_NOPE_DIM + QK_ROPE_DIM) ** -0.5),
        out_shape=[jax.ShapeDtypeStruct((t, qw), BF16), jax.ShapeDtypeStruct((t, MLA_WIDTH), BF16),
                   jax.ShapeDtypeStruct((t, LANE), BF16), jax.ShapeDtypeStruct((t, MLA_WIDTH), BF16)],
        grid=(t // tm,),
        in_specs=[pl.BlockSpec((tm, Q_LORA_RANK), lambda i: (i, OFF_CQ // Q_LORA_RANK)),
                  pl.BlockSpec((tm, KV_LORA_RANK), lambda i: (i, OFF_CKV // KV_LORA_RANK)),
                  pl.BlockSpec((tm, LANE), lambda i: (i, OFF_KR // LANE)),
                  pl.BlockSpec((1, Q_LORA_RANK), const), pl.BlockSpec((1, KV_LORA_RANK), const),
                  pl.BlockSpec((Q_LORA_RANK, qw), const), pl.BlockSpec((Q_LORA_RANK, qw), const),
                  pl.BlockSpec((KV_LORA_RANK, MLA_WIDTH), const),
                  pl.BlockSpec((KV_LORA_RANK, MLA_WIDTH), const),
                  pl.BlockSpec((tm, Q_SLOT), pos), pl.BlockSpec((tm, Q_SLOT), pos),
                  pl.BlockSpec((tm, LANE), pos)],
        out_specs=[pl.BlockSpec((tm, qw), lambda i: (i, 0)), pl.BlockSpec((tm, MLA_WIDTH), lambda i: (i, 0)),
                   pl.BlockSpec((tm, LANE), lambda i: (i, 0)), pl.BlockSpec((tm, MLA_WIDTH), lambda i: (i, 0))],
        compiler_params=_cparams(("parallel",)),
        name="mla_prep",
    )(proj, proj, proj, lw["q_norm_g"], lw["kv_norm_g"], lw["w_qa"], lw["w_qb"], lw["w_kn"], lw["w_v"],
      tabs["ct"], tabs["st"], tabs["kt"])


def _flash_kernel(q_ref, kn_ref, kr_ref, v_ref, o_ref, m_sc, l_sc, acc_sc, *, tq, tk):
    i, j = pl.program_id(2), pl.program_id(3)

    @pl.when(j == 0)
    def _():
        m_sc[...] = jnp.full_like(m_sc, -jnp.inf)
        l_sc[...] = jnp.zeros_like(l_sc)
        acc_sc[...] = jnp.zeros_like(acc_sc)

    def step(masked):
        kf = jnp.concatenate([kn_ref[0], kr_ref[0]], axis=1)
        s = _dot_nt(q_ref[0], kf)
        if masked:
            qpos = i * tq + lax.broadcasted_iota(jnp.int32, (tq, tk), 0)
            kpos = j * tk + lax.broadcasted_iota(jnp.int32, (tq, tk), 1)
            s = jnp.where(qpos >= kpos, s, jnp.finfo(F32).min)
        m_prev = m_sc[...]
        m_new = jnp.maximum(m_prev, jnp.max(s, axis=-1, keepdims=True))
        alpha = jnp.exp(m_prev - m_new)
        p = jnp.exp(s - m_new)
        l_sc[...] = alpha * l_sc[...] + jnp.sum(p, axis=-1, keepdims=True)
        acc_sc[...] = alpha * acc_sc[...] + _dot(p.astype(BF16), v_ref[0])
        m_sc[...] = m_new

    needed = j * tk < (i + 1) * tq
    straddles = (j + 1) * tk - 1 > i * tq

    @pl.when(jnp.logical_and(needed, straddles))
    def _():
        step(True)

    @pl.when(jnp.logical_and(needed, jnp.logical_not(straddles)))
    def _():
        step(False)

    @pl.when(j == pl.num_programs(3) - 1)
    def _():
        o_ref[0] = (acc_sc[...] / l_sc[...]).astype(o_ref.dtype)


def _flash_attention(q, kn, kr, v, *, tq, tk):
    bsz, seq, _ = q.shape
    last_kv = lambda i: ((i + 1) * tq - 1) // tk
    kv_map = lambda b, h, i, j: (b, jnp.minimum(j, last_kv(i)), h)
    return pl.pallas_call(
        functools.partial(_flash_kernel, tq=tq, tk=tk),
        out_shape=jax.ShapeDtypeStruct((bsz, seq, MLA_WIDTH), BF16),
        grid=(bsz, MLA_HEADS, seq // tq, seq // tk),
        in_specs=[pl.BlockSpec((1, tq, Q_SLOT), lambda b, h, i, j: (b, i, h)),
                  pl.BlockSpec((1, tk, QK_NOPE_DIM), kv_map),
                  pl.BlockSpec((1, tk, LANE), lambda b, h, i, j: (b, jnp.minimum(j, last_kv(i)), 0)),
                  pl.BlockSpec((1, tk, V_HEAD_DIM), kv_map)],
        out_specs=pl.BlockSpec((1, tq, V_HEAD_DIM), lambda b, h, i, j: (b, i, h)),
        scratch_shapes=[pltpu.VMEM((tq, 1), F32), pltpu.VMEM((tq, 1), F32),
                        pltpu.VMEM((tq, V_HEAD_DIM), F32)],
        compiler_params=_cparams(("parallel", "parallel", "parallel", "arbitrary")),
        name="mla_flash",
    )(q, kn, kr, v)


def _out_proj_kernel(yr_ref, ym_ref, wr_ref, wm_ref, h_ref, g_ref, o_ref):
    y = _dot(yr_ref[...], wr_ref[...]) + _dot(ym_ref[...], wm_ref[...])
    o_ref[...] = h_ref[...] + _rms(y, g_ref[...])


def _out_proj(yr, ym, wr, wm, h, g, *, tm):
    t, d = h.shape
    half = yr.shape[1]
    row = lambda i: (i, 0)
    const = lambda i: (0, 0)
    return pl.pallas_call(
        _out_proj_kernel,
        out_shape=jax.ShapeDtypeStruct((t, d), F32),
        grid=(t // tm,),
        in_specs=[pl.BlockSpec((tm, half), row), pl.BlockSpec((tm, half), row),
                  pl.BlockSpec((half, d), const), pl.BlockSpec((half, d), const),
                  pl.BlockSpec((tm, d), row), pl.BlockSpec((1, d), const)],
        out_specs=pl.BlockSpec((tm, d), row),
        compiler_params=_cparams(("parallel",)),
        name="out_proj",
    )(yr, ym, wr, wm, h, g)


def _mem_attn_kernel(h_ref, gpre_ref, wq_ref, k_ref, v_ref, wo_ref, gpost_ref, o_ref):
    h = h_ref[...]
    u = _rms(h, gpre_ref[...]).astype(BF16)
    q = (_dot(u, wq_ref[...]) * (MEM_HEAD_DIM ** -0.5)).astype(BF16)
    outs = []
    for hd in range(MEM_HEADS):
        sl = slice(hd * MEM_HEAD_DIM, (hd + 1) * MEM_HEAD_DIM)
        s = _dot_nt(q[:, sl], k_ref[0][:, sl])
        p = jnp.exp(s - jnp.max(s, axis=-1, keepdims=True))
        denom = jnp.sum(p, axis=-1, keepdims=True)
        p = (p / denom).astype(BF16)
        outs.append(_dot(p, v_ref[0][:, sl]))
    o = jnp.concatenate(outs, axis=1).astype(BF16)
    y = _dot(o, wo_ref[...])
    o_ref[...] = h + _rms(y, gpost_ref[...])


def _mem_attn(h, gpre, wq, kmem, vmem, wo, gpost, *, seq, tm):
    t, d = h.shape
    nmem, mw = kmem.shape[1], kmem.shape[2]
    sb = seq // tm
    row = lambda i: (i, 0)
    const = lambda i: (0, 0)
    bat = lambda i: (i // sb, 0, 0)
    return pl.pallas_call(
        _mem_attn_kernel,
        out_shape=jax.ShapeDtypeStruct((t, d), F32),
        grid=(t // tm,),
        in_specs=[pl.BlockSpec((tm, d), row), pl.BlockSpec((1, d), const), pl.BlockSpec((d, mw), const),
                  pl.BlockSpec((1, nmem, mw), bat), pl.BlockSpec((1, nmem, mw), bat),
                  pl.BlockSpec((mw, d), const), pl.BlockSpec((1, d), const)],
        out_specs=pl.BlockSpec((tm, d), row),
        compiler_params=_cparams(("parallel",)),
        name="mem_attn",
    )(h, gpre, wq, kmem, vmem, wo, gpost)


def _ffn_kernel(h_ref, gpre_ref, w1_ref, w2_ref, gpost_ref, o_ref, u_ref, acc_ref):
    j = pl.program_id(1)

    @pl.when(j == 0)
    def _():
        u_ref[...] = _rms(h_ref[...], gpre_ref[...]).astype(BF16)
        acc_ref[...] = jnp.zeros_like(acc_ref)

    a = jnp.maximum(_dot(u_ref[...], w1_ref[...]), 0.0)
    acc_ref[...] += _dot((a * a).astype(BF16), w2_ref[...])

    @pl.when(j == pl.num_programs(1) - 1)
    def _():
        o_ref[...] = h_ref[...] + _rms(acc_ref[...], gpost_ref[...])


def _ffn(h, gpre, w1, w2, gpost, *, tm, tf):
    t, d = h.shape
    dff = w1.shape[1]
    row = lambda i, j: (i, 0)
    const = lambda i, j: (0, 0)
    return pl.pallas_call(
        _ffn_kernel,
        out_shape=jax.ShapeDtypeStruct((t, d), F32),
        grid=(t // tm, dff // tf),
        in_specs=[pl.BlockSpec((tm, d), row), pl.BlockSpec((1, d), const),
                  pl.BlockSpec((d, tf), lambda i, j: (0, j)), pl.BlockSpec((tf, d), lambda i, j: (j, 0)),
                  pl.BlockSpec((1, d), const)],
        out_specs=pl.BlockSpec((tm, d), row),
        scratch_shapes=[pltpu.VMEM((tm, d), BF16), pltpu.VMEM((tm, d), F32)],
        compiler_params=_cparams(("parallel", "arbitrary")),
        name="ffn",
    )(h, gpre, w1, w2, gpost)


def _pad_cols(w, n):
    return jnp.pad(w, ((0, 0), (0, n - w.shape[1])))


def _pad_rows(w, n):
    return jnp.pad(w, ((0, n - w.shape[0]), (0, 0)))


def _pad_vec(v, n):
    return jnp.pad(v, (0, n - v.shape[0])).reshape(1, n)


def _layer_weights(l, w_in, w_in_vres, mu_rwkv, mu_vres, w0, w_up, a0, a_up, v0, v_up, g_up, k_k, k_a,
                   q_norm_g, w_uq, kv_norm_g, w_ukv):
    c = RWKV_WIDTH
    wi = w_in[l]
    mla0 = 3 * c + DECAY_LORA + ICLR_LORA + GATE_LORA
    half = QK_ROPE_DIM // 2
    kr0 = mla0 + Q_LORA_RANK + KV_LORA_RANK
    kr1, kr2 = wi[:, kr0:kr0 + half], wi[:, kr0 + half:kr0 + QK_ROPE_DIM]
    vres = w_in_vres[l - 1] if l > 0 else jnp.zeros((wi.shape[0], VRES_LORA), wi.dtype)
    w_all = jnp.concatenate([
        wi[:, :3 * c],
        _pad_cols(wi[:, 3 * c:3 * c + DECAY_LORA], LANE),
        _pad_cols(wi[:, 3 * c + DECAY_LORA:3 * c + DECAY_LORA + ICLR_LORA], LANE),
        wi[:, 3 * c + DECAY_LORA + ICLR_LORA:mla0],
        wi[:, mla0:kr0],
        kr1, kr2, kr2, kr1,
        _pad_cols(vres, LANE)], axis=1).astype(BF16)
    mu = mu_rwkv[l]
    mu_all = jnp.concatenate([
        mu[:3 * c], jnp.pad(mu[3 * c:3 * c + DECAY_LORA], (0, LANE - DECAY_LORA)),
        jnp.pad(mu[3 * c + DECAY_LORA:3 * c + DECAY_LORA + ICLR_LORA], (0, LANE - ICLR_LORA)),
        mu[3 * c + DECAY_LORA + ICLR_LORA:]]).reshape(1, RWKV_COLS)

    qd = QK_NOPE_DIM + QK_ROPE_DIM
    wq = w_uq[l].reshape(Q_LORA_RANK, MLA_HEADS, qd)
    qn, q1, q2 = wq[..., :QK_NOPE_DIM], wq[..., QK_NOPE_DIM:QK_NOPE_DIM + half], wq[..., QK_NOPE_DIM + half:]
    zpad = jnp.zeros((Q_LORA_RANK, MLA_HEADS, Q_SLOT - qd), wq.dtype)
    w_qa = jnp.concatenate([qn, q1, q2, zpad], axis=-1).reshape(Q_LORA_RANK, -1).astype(BF16)
    w_qb = jnp.concatenate([jnp.zeros_like(qn), q2, q1, zpad], axis=-1).reshape(Q_LORA_RANK, -1).astype(BF16)
    wkv = w_ukv[l].reshape(KV_LORA_RANK, MLA_HEADS, QK_NOPE_DIM + V_HEAD_DIM)
    lw = {
        "w_all": w_all, "mu": mu_all,
        "w0": w0[l].reshape(1, c), "a0": a0[l].reshape(1, c),
        "w_up": _pad_rows(w_up[l], LANE).astype(BF16), "a_up": _pad_rows(a_up[l], LANE).astype(BF16),
        "g_up": g_up[l].astype(BF16), "k_k": k_k[l].reshape(1, c), "k_a": k_a[l].reshape(1, c),
        "q_norm_g": q_norm_g[l].reshape(1, -1), "kv_norm_g": kv_norm_g[l].reshape(1, -1),
        "w_qa": w_qa, "w_qb": w_qb,
        "w_kn": wkv[..., :QK_NOPE_DIM].reshape(KV_LORA_RANK, -1).astype(BF16),
        "w_v": wkv[..., QK_NOPE_DIM:].reshape(KV_LORA_RANK, -1).astype(BF16),
    }
    if l > 0:
        lw["mu_vres"] = _pad_vec(mu_vres[l - 1], LANE)
        lw["v0"] = v0[l - 1].reshape(1, c)
        lw["v_up"] = _pad_rows(v_up[l - 1], LANE).astype(BF16)
    return lw


def _rope_tabs(seq):
    pos = jnp.arange(seq, dtype=F32)
    inv_freq = ROPE_THETA ** (-jnp.arange(0, QK_ROPE_DIM, 2, dtype=F32) / QK_ROPE_DIM)
    ang = pos[:, None] * inv_freq[None, :]
    cos, sin = jnp.cos(ang), jnp.sin(ang)
    one = jnp.ones((seq, QK_NOPE_DIM), F32)
    zero_n = jnp.zeros((seq, QK_NOPE_DIM), F32)
    zero_p = jnp.zeros((seq, Q_SLOT - QK_NOPE_DIM - QK_ROPE_DIM), F32)
    return {
        "ct": jnp.concatenate([one, cos, cos, zero_p], axis=1),
        "st": jnp.concatenate([zero_n, -sin, sin, zero_p], axis=1),
        "kt": jnp.concatenate([cos, cos, -sin, sin], axis=1),
    }


def _tile(n, pref):
    t = min(n, pref)
    assert n % t == 0, (n, t)
    return t


def kernel(x, mem, mem_norm_g, mix_pre_g, w_in, w_in_vres, mu_rwkv, mu_vres, w0, w_up, a0, a_up, v0, v_up, g_up, k_k, k_a, r_k, lnx_g, lnx_b, q_norm_g, w_uq, kv_norm_g, w_ukv, w_out, mix_post_g, mem_pre_g, wq_mem, wk_mem, wv_mem, wo_mem, mem_post_g, ffn_pre_g, w_ff1, w_ff2, ffn_post_g):
    bsz, seq, d = x.shape
    depth = w_in.shape[0]
    t = bsz * seq
    nmem = mem.shape[1]
    mw = wq_mem.shape[2]
    assert seq % SCAN_CHUNK == 0 and d % LANE == 0
    tm_proj = _tile(seq, 512)
    tm_prep = _tile(seq, 256)
    tm_row = _tile(seq, 512)
    tq = _tile(seq, 512)

    tabs = _rope_tabs(seq)
    w_kv_mem = jnp.concatenate([jnp.concatenate([wk_mem[l], wv_mem[l]], axis=1) for l in range(depth)],
                               axis=1).astype(BF16)
    kv_mem = _norm_matmul(mem.reshape(bsz * nmem, d), mem_norm_g, w_kv_mem,
                          tm=_tile(bsz * nmem, 256), tn=_tile(w_kv_mem.shape[1], 512))
    kv_mem = kv_mem.astype(BF16).reshape(bsz, nmem, depth, 2, mw)

    h = x.reshape(t, d)
    v_first = None
    for l in range(depth):
        lw = _layer_weights(l, w_in, w_in_vres, mu_rwkv, mu_vres, w0, w_up, a0, a_up, v0, v_up, g_up,
                            k_k, k_a, q_norm_g, w_uq, kv_norm_g, w_ukv)
        proj = _norm_matmul(h, mix_pre_g[l], lw["w_all"], tm=tm_proj, tn=_tile(PROJ_COLS, 1536))
        r, k, v, kk, b, ld, g = _rwkv_prep(proj, lw, v_first, seq=seq, tm=tm_prep)
        if l == 0:
            v_first = v
        to3 = lambda a: a.reshape(bsz, seq, RWKV_WIDTH)
        y_rwkv = _rwkv_scan(to3(r), to3(k), to3(v), to3(kk), to3(b), to3(ld), to3(g),
                            r_k[l].reshape(1, -1), lnx_g[l].reshape(1, -1), lnx_b[l].reshape(1, -1))
        q, kn, kr, vv = _mla_prep(proj, lw, tabs, seq=seq, tm=tm_row)
        y_mla = _flash_attention(q.reshape(bsz, seq, -1), kn.reshape(bsz, seq, -1),
                                 kr.reshape(bsz, seq, -1), vv.reshape(bsz, seq, -1), tq=tq, tk=tq)
        wo = w_out[l].astype(BF16)
        h = _out_proj(y_rwkv.reshape(t, -1), y_mla.reshape(t, -1), wo[:RWKV_WIDTH], wo[RWKV_WIDTH:],
                      h, mix_post_g[l].reshape(1, d), tm=tm_row)
        h = _mem_attn(h, mem_pre_g[l].reshape(1, d), wq_mem[l].astype(BF16), kv_mem[:, :, l, 0],
                      kv_mem[:, :, l, 1], wo_mem[l].astype(BF16), mem_post_g[l].reshape(1, d),
                      seq=seq, tm=tm_row)
        h = _ffn(h, ffn_pre_g[l].reshape(1, d), w_ff1[l].astype(BF16), w_ff2[l].astype(BF16),
                 ffn_post_g[l].reshape(1, d), tm=tm_row, tf=512)
    return h.reshape(bsz, seq, d)
```

```python
import functools

import jax
import jax.numpy as jnp
from jax import lax
from jax.experimental import pallas as pl
from jax.experimental.pallas import tpu as pltpu

F32 = jnp.float32
BF16 = jnp.bfloat16

NORM_EPS = 1e-6
RWKV_GN_EPS = 64e-5
ROPE_THETA = 10000.0
LOG2_E = 1.4426950408889634

RWKV_HEAD_DIM = 64
RWKV_WIDTH = 1024
DECAY_LORA = 96
ICLR_LORA = 96
GATE_LORA = 256
VRES_LORA = 64

MLA_HEADS = 8
QK_NOPE_DIM = 128
QK_ROPE_DIM = 64
V_HEAD_DIM = 128
Q_LORA_RANK = 512
KV_LORA_RANK = 256
MLA_WIDTH = MLA_HEADS * V_HEAD_DIM
Q_SLOT = 256

MEM_HEADS = 4
MEM_HEAD_DIM = 128

LANE = 128
SCAN_CHUNK = 64
SCAN_HEADS = 4
SCAN_TILE = SCAN_HEADS * RWKV_HEAD_DIM

OFF_R, OFF_K, OFF_V = 0, 1024, 2048
OFF_WL, OFF_AL, OFF_GL = 3072, 3200, 3328
RWKV_COLS = 3584
OFF_CQ, OFF_CKV, OFF_KR, OFF_VRES = 3584, 4096, 4352, 4480
PROJ_COLS = 4608

VMEM_LIMIT = 56 * 1024 * 1024


def _cparams(sem):
    return pltpu.CompilerParams(dimension_semantics=sem, vmem_limit_bytes=VMEM_LIMIT)


def _rms(x, g):
    ms = jnp.mean(x * x, axis=-1, keepdims=True)
    return x * lax.rsqrt(ms + NORM_EPS) * g


def _dot(a, b):
    return jnp.dot(a, b, preferred_element_type=F32)


def _dot_nt(a, b):
    return lax.dot_general(a, b, (((1,), (1,)), ((), ())), preferred_element_type=F32)


def _dot_tn(a, b):
    return lax.dot_general(a, b, (((0,), (0,)), ((), ())), preferred_element_type=F32)


def _split_dot(x, w):
    hi = x.astype(BF16)
    lo = (x - hi.astype(F32)).astype(BF16)
    return _dot(hi, w) + _dot(lo, w)


def _sigmoid(x):
    return 1.0 / (1.0 + jnp.exp(-x))


def _softplus(x):
    return jnp.maximum(x, 0.0) + jnp.log(1.0 + jnp.exp(-jnp.abs(x)))


def _group_ones(n, group):
    r = lax.broadcasted_iota(jnp.int32, (n, n), 0) // group
    c = lax.broadcasted_iota(jnp.int32, (n, n), 1) // group
    return (r == c).astype(BF16)


def _head_sum(x, ones):
    w = ones.shape[0]
    return jnp.concatenate(
        [_split_dot(x[:, c * w:(c + 1) * w], ones) for c in range(x.shape[1] // w)], axis=1)


def _norm_matmul_kernel(h_ref, g_ref, w_ref, o_ref, u_ref):
    @pl.when(pl.program_id(1) == 0)
    def _():
        u_ref[...] = _rms(h_ref[...], g_ref[...]).astype(BF16)

    o_ref[...] = _dot(u_ref[...], w_ref[...])


def _norm_matmul(h, g, w, *, tm, tn):
    t, d = h.shape
    n = w.shape[1]
    return pl.pallas_call(
        _norm_matmul_kernel,
        out_shape=jax.ShapeDtypeStruct((t, n), F32),
        grid=(t // tm, n // tn),
        in_specs=[pl.BlockSpec((tm, d), lambda i, j: (i, 0)),
                  pl.BlockSpec((1, d), lambda i, j: (0, 0)),
                  pl.BlockSpec((d, tn), lambda i, j: (0, j))],
        out_specs=pl.BlockSpec((tm, tn), lambda i, j: (i, j)),
        scratch_shapes=[pltpu.VMEM((tm, d), BF16)],
        compiler_params=_cparams(("parallel", "arbitrary")),
        name="norm_matmul",
    )(h, g.reshape(1, d), w)


def _shift_mix(y, prev8, mu, first):
    prev_row = jnp.where(first, 0.0, prev8[7:8, :])
    y_prev = pltpu.roll(y, 1, 0)
    row = lax.broadcasted_iota(jnp.int32, y.shape, 0)
    y_prev = jnp.where(row == 0, prev_row, y_prev)
    return y + (y_prev - y) * mu


def _rwkv_prep_kernel(*refs, seq_blocks, has_vres):
    if has_vres:
        (y_ref, yp_ref, mu_ref, w0_ref, wup_ref, a0_ref, aup_ref, gup_ref, kk_ref, ka_ref,
         yv_ref, yvp_ref, muv_ref, v0_ref, vup_ref, vf_ref,
         r_o, k_o, v_o, kkn_o, b_o, ld_o, g_o) = refs
    else:
        (y_ref, yp_ref, mu_ref, w0_ref, wup_ref, a0_ref, aup_ref, gup_ref, kk_ref, ka_ref,
         r_o, k_o, v_o, kkn_o, b_o, ld_o, g_o) = refs
    first = (pl.program_id(0) % seq_blocks) == 0
    c = _shift_mix(y_ref[...], yp_ref[...], mu_ref[...], first)
    r = c[:, OFF_R:OFF_R + RWKV_WIDTH]
    k = c[:, OFF_K:OFF_K + RWKV_WIDTH]
    v = c[:, OFF_V:OFF_V + RWKV_WIDTH]
    wl = c[:, OFF_WL:OFF_WL + LANE]
    al = c[:, OFF_AL:OFF_AL + LANE]
    gl = c[:, OFF_GL:OFF_GL + GATE_LORA]

    z = w0_ref[...] + _dot(jnp.tanh(wl).astype(BF16), wup_ref[...])
    log_w = -_softplus(-z) - 0.5
    ld_o[...] = -jnp.exp(log_w)
    a = _sigmoid(a0_ref[...] + _dot(al.astype(BF16), aup_ref[...]))
    g_o[...] = _dot(_sigmoid(gl).astype(BF16), gup_ref[...])
    if has_vres:
        vr = _shift_mix(yv_ref[...], yvp_ref[...], muv_ref[...], first)
        mix = _sigmoid(v0_ref[...] + _dot(vr.astype(BF16), vup_ref[...]))
        v = v + (vf_ref[...] - v) * mix
    ones = _group_ones(LANE, RWKV_HEAD_DIM)
    kk = k * kk_ref[...]
    kk = kk * lax.rsqrt(jnp.maximum(_head_sum(kk * kk, ones), 1e-24))
    r_o[...] = r
    k_o[...] = k * (1.0 + (a - 1.0) * ka_ref[...])
    v_o[...] = v
    kkn_o[...] = kk
    b_o[...] = kk * a


def _rwkv_prep(proj, lw, v_first, *, seq, tm):
    t = proj.shape[0]
    has_vres = v_first is not None
    cw = RWKV_WIDTH
    row = lambda i: (i, 0)
    prev = lambda i: (jnp.maximum(i * (tm // 8) - 1, 0), 0)
    const = lambda i: (0, 0)
    vec = pl.BlockSpec((1, cw), const)
    in_specs = [pl.BlockSpec((tm, RWKV_COLS), row), pl.BlockSpec((8, RWKV_COLS), prev),
                pl.BlockSpec((1, RWKV_COLS), const), vec, pl.BlockSpec((LANE, cw), const),
                vec, pl.BlockSpec((LANE, cw), const), pl.BlockSpec((GATE_LORA, cw), const), vec, vec]
    args = [proj, proj, lw["mu"], lw["w0"], lw["w_up"], lw["a0"], lw["a_up"], lw["g_up"],
            lw["k_k"], lw["k_a"]]
    if has_vres:
        vres_blk = OFF_VRES // LANE
        in_specs += [pl.BlockSpec((tm, LANE), lambda i: (i, vres_blk)),
                     pl.BlockSpec((8, LANE), lambda i: (jnp.maximum(i * (tm // 8) - 1, 0), vres_blk)),
                     pl.BlockSpec((1, LANE), const), vec, pl.BlockSpec((LANE, cw), const),
                     pl.BlockSpec((tm, cw), row)]
        args += [proj, proj, lw["mu_vres"], lw["v0"], lw["v_up"], v_first]
    out = jax.ShapeDtypeStruct((t, cw), F32)
    return pl.pallas_call(
        functools.partial(_rwkv_prep_kernel, seq_blocks=seq // tm, has_vres=has_vres),
        out_shape=[out] * 7,
        grid=(t // tm,),
        in_specs=in_specs,
        out_specs=[pl.BlockSpec((tm, cw), row)] * 7,
        compiler_params=_cparams(("parallel",)),
        name="rwkv_prep",
    )(*args)


def _rwkv_scan_kernel(r_ref, k_ref, v_ref, kk_ref, b_ref, ld_ref, g_ref, rk_ref, lg_ref, lb_ref,
                      y_ref, ht_ref, *, nbatch):
    cs, n = SCAN_CHUNK, SCAN_TILE

    @pl.when(pl.program_id(1) == 0)
    def _():
        ht_ref[...] = jnp.zeros_like(ht_ref)

    ri = lax.broadcasted_iota(jnp.int32, (n, n), 0)
    ci = lax.broadcasted_iota(jnp.int32, (n, n), 1)
    same_head = (ri // RWKV_HEAD_DIM) == (ci // RWKV_HEAD_DIM)
    strict = (ri % cs) > (ci % cs)
    incl = (ri % cs) >= (ci % cs)
    eye = (ri == ci).astype(F32)
    ones_bd = same_head.astype(BF16)
    tr = lax.broadcasted_iota(jnp.int32, (cs, cs), 0)
    tc = lax.broadcasted_iota(jnp.int32, (cs, cs), 1)
    tri = (tr >= tc).astype(BF16)
    sub_masks = []
    s = 1
    while s < cs:
        sub_masks.append(((ri // (2 * s)) == (ci // (2 * s))) & ((ri % (2 * s)) >= s) & ((ci % (2 * s)) < s))
        s *= 2

    def bd(x):
        return jnp.where(same_head, jnp.concatenate([x] * SCAN_HEADS, axis=0), 0.0).astype(BF16)

    rows = range(nbatch)
    r, k, v = [r_ref[i] for i in rows], [k_ref[i] for i in rows], [v_ref[i] for i in rows]
    kk, b, ld = [kk_ref[i] for i in rows], [b_ref[i] for i in rows], [ld_ref[i] for i in rows]
    ld_hi = [x.astype(BF16) for x in ld]
    cum = [_dot(tri, h) + _dot(tri, (x - h.astype(F32)).astype(BF16)) for x, h in zip(ld, ld_hi)]
    cum_c = [c[cs - 1:cs, :] for c in cum]
    ginv = [jnp.exp(-c) for c in cum]
    gtail = [jnp.exp(cc - c) for c, cc in zip(cum, cum_c)]
    a_bd = [bd(-kk[i] * jnp.exp(cum[i] - ld[i])) for i in rows]
    r_bd = [bd(r[i] * jnp.exp(cum[i])) for i in rows]
    b_bd = [bd(b[i] * ginv[i]) for i in rows]
    k_bd = [bd(k[i] * ginv[i]) for i in rows]
    v_bd = [bd(x) for x in v]

    sc = [_dot_nt(jnp.concatenate([a_bd[i], r_bd[i]], axis=0), jnp.concatenate([b_bd[i], k_bd[i]], axis=0))
          for i in rows]
    l_ab = [s[:n, :n] for s in sc]
    l_ak = [jnp.where(strict, s[:n, n:], 0.0).astype(BF16) for s in sc]
    m_rb = [jnp.where(incl, s[n:, :n], 0.0).astype(BF16) for s in sc]
    m_rk = [jnp.where(incl, s[n:, n:], 0.0).astype(BF16) for s in sc]

    t_inv = [eye + jnp.where(sub_masks[0], x, 0.0) for x in l_ab]
    for sm in sub_masks[1:]:
        t_b = [t.astype(BF16) for t in t_inv]
        x = [_dot(jnp.where(sm, l_ab[i], 0.0).astype(BF16), t_b[i]).astype(BF16) for i in rows]
        t_inv = [t_inv[i] + _dot(t_b[i], x[i]) for i in rows]

    lv = [_dot(l_ak[i], v_bd[i]).astype(BF16) for i in rows]
    tx = [_dot(t_inv[i].astype(BF16), jnp.concatenate([a_bd[i], lv[i]], axis=1)) for i in rows]
    ht = [ht_ref[i] for i in rows]
    ht_b = [h.astype(BF16) for h in ht]
    u = [(_dot_nt(tx[i][:, :n].astype(BF16), ht_b[i]) + tx[i][:, n:]).astype(BF16) for i in rows]
    o_bd = [_dot_nt(r_bd[i], ht_b[i]) + _dot(m_rb[i], u[i]) + _dot(m_rk[i], v_bd[i]) for i in rows]
    for i in rows:
        ht_ref[i] = (ht[i] * jnp.exp(cum_c[i]) + _dot_tn(u[i], bd(b[i] * gtail[i]))
                     + _dot_tn(v_bd[i], bd(k[i] * gtail[i])))

    inv_n = 1.0 / RWKV_HEAD_DIM
    for i in rows:
        ob = o_bd[i]
        o = ob[0:cs] + ob[cs:2 * cs] + ob[2 * cs:3 * cs] + ob[3 * cs:4 * cs]
        mean = _split_dot(o, ones_bd) * inv_n
        d = o - mean
        var = _split_dot(d * d, ones_bd) * inv_n
        on = d * lax.rsqrt(var + RWKV_GN_EPS) * lg_ref[...] + lb_ref[...]
        bonus = _split_dot(r[i] * k[i] * rk_ref[...], ones_bd) * v[i]
        y_ref[i] = ((on + bonus) * g_ref[i]).astype(y_ref.dtype)


def _rwkv_scan(r, k, v, kk, b, ld, g, r_k, lnx_g, lnx_b):
    bsz, seq, cw = r.shape
    blk = pl.BlockSpec((bsz, SCAN_CHUNK, SCAN_TILE), lambda h, c: (0, c, h))
    vec = pl.BlockSpec((1, SCAN_TILE), lambda h, c: (0, h))
    return pl.pallas_call(
        functools.partial(_rwkv_scan_kernel, nbatch=bsz),
        out_shape=jax.ShapeDtypeStruct((bsz, seq, cw), BF16),
        grid=(cw // SCAN_TILE, seq // SCAN_CHUNK),
        in_specs=[blk] * 7 + [vec] * 3,
        out_specs=blk,
        scratch_shapes=[pltpu.VMEM((bsz, SCAN_TILE, SCAN_TILE), F32)],
        compiler_params=_cparams(("parallel", "arbitrary")),
        name="rwkv_scan",
    )(r, k, v, kk, b, ld, g, r_k, lnx_g, lnx_b)


def _mla_prep_kernel(cq_ref, ckv_ref, kr_ref, gq_ref, gkv_ref, wa_ref, wb_ref, wkn_ref, wv_ref,
                     ct_ref, st_ref, kt_ref, q_o, kn_o, kr_o, v_o, *, scale):
    cqn = _rms(cq_ref[...], gq_ref[...]).astype(BF16)
    qa = _dot(cqn, wa_ref[...])
    qb = _dot(cqn, wb_ref[...])
    ct = ct_ref[...] * scale
    st = st_ref[...] * scale
    for h in range(MLA_HEADS):
        sl = slice(h * Q_SLOT, (h + 1) * Q_SLOT)
        q_o[:, sl] = (qa[:, sl] * ct + qb[:, sl] * st).astype(BF16)
    ckvn = _rms(ckv_ref[...], gkv_ref[...]).astype(BF16)
    kn_o[...] = _dot(ckvn, wkn_ref[...]).astype(BF16)
    v_o[...] = _dot(ckvn, wv_ref[...]).astype(BF16)
    kr = kr_ref[...] * kt_ref[...]
    kr = kr + pltpu.roll(kr, QK_ROPE_DIM, 1)
    lane = lax.broadcasted_iota(jnp.int32, kr.shape, 1)
    kr_o[...] = jnp.where(lane < QK_ROPE_DIM, kr, 0.0).astype(BF16)


def _mla_prep(proj, lw, tabs, *, seq, tm):
    t = proj.shape[0]
    sb = seq // tm
    const = lambda i: (0, 0)
    pos = lambda i: (i % sb, 0)
    qw = MLA_HEADS * Q_SLOT
    return pl.pallas_call(
        functools.partial(_mla_prep_kernel, scale=LOG2_E * (QK_NOPE_DIM + QK_ROPE_DIM) ** -0.5),
        out_shape=[jax.ShapeDtypeStruct((t, qw), BF16), jax.ShapeDtypeStruct((t, MLA_WIDTH), BF16),
                   jax.ShapeDtypeStruct((t, LANE), BF16), jax.ShapeDtypeStruct((t, MLA_WIDTH), BF16)],
        grid=(t // tm,),
        in_specs=[pl.BlockSpec((tm, Q_LORA_RANK), lambda i: (i, OFF_CQ // Q_LORA_RANK)),
                  pl.BlockSpec((tm, KV_LORA_RANK), lambda i: (i, OFF_CKV // KV_LORA_RANK)),
                  pl.BlockSpec((tm, LANE), lambda i: (i, OFF_KR // LANE)),
                  pl.BlockSpec((1, Q_LORA_RANK), const), pl.BlockSpec((1, KV_LORA_RANK), const),
                  pl.BlockSpec((Q_LORA_RANK, qw), const), pl.BlockSpec((Q_LORA_RANK, qw), const),
                  pl.BlockSpec((KV_LORA_RANK, MLA_WIDTH), const),
                  pl.BlockSpec((KV_LORA_RANK, MLA_WIDTH), const),
                  pl.BlockSpec((tm, Q_SLOT), pos), pl.BlockSpec((tm, Q_SLOT), pos),
                  pl.BlockSpec((tm, LANE), pos)],
        out_specs=[pl.BlockSpec((tm, qw), lambda i: (i, 0)), pl.BlockSpec((tm, MLA_WIDTH), lambda i: (i, 0)),
                   pl.BlockSpec((tm, LANE), lambda i: (i, 0)), pl.BlockSpec((tm, MLA_WIDTH), lambda i: (i, 0))],
        compiler_params=_cparams(("parallel",)),
        name="mla_prep",
    )(proj, proj, proj, lw["q_norm_g"], lw["kv_norm_g"], lw["w_qa"], lw["w_qb"], lw["w_kn"], lw["w_v"],
      tabs["ct"], tabs["st"], tabs["kt"])


def _flash_kernel(q_ref, kn_ref, kr_ref, v_ref, o_ref, *, tq, nheads):
    i = pl.program_id(2)
    q = [q_ref[0, :, h * Q_SLOT:(h + 1) * Q_SLOT] for h in range(nheads)]

    def block(j, carry, masked):
        off = pl.multiple_of(j * tq, tq)
        kr = kr_ref[0, pl.ds(off, tq), :]
        out = []
        for h in range(nheads):
            m_prev, l_prev, acc = carry[h]
            kf = jnp.concatenate([kn_ref[0, pl.ds(off, tq), h * LANE:(h + 1) * LANE], kr], axis=1)
            s = _dot_nt(q[h], kf)
            if masked:
                row = lax.broadcasted_iota(jnp.int32, (tq, tq), 0)
                col = lax.broadcasted_iota(jnp.int32, (tq, tq), 1)
                s = jnp.where(row >= col, s, jnp.finfo(F32).min)
            m_new = jnp.maximum(m_prev, jnp.max(s, axis=-1, keepdims=True))
            alpha = jnp.exp2(m_prev - m_new)
            p = jnp.exp2(s - m_new)
            l_new = alpha * l_prev + jnp.sum(p, axis=-1, keepdims=True)
            pv = _dot(p.astype(BF16), v_ref[0, pl.ds(off, tq), h * LANE:(h + 1) * LANE])
            out.append((m_new, l_new, alpha * acc + pv))
        return tuple(out)

    init = tuple((jnp.full((tq, 1), -jnp.inf, F32), jnp.zeros((tq, 1), F32),
                  jnp.zeros((tq, V_HEAD_DIM), F32)) for _ in range(nheads))
    carry = lax.fori_loop(0, i, lambda j, c: block(j, c, False), init)
    carry = block(i, carry, True)
    for h in range(nheads):
        _, l_fin, acc = carry[h]
        o_ref[0, :, h * LANE:(h + 1) * LANE] = (acc / l_fin).astype(o_ref.dtype)


def _flash_attention(q, kn, kr, v, *, tq, nheads):
    bsz, seq, _ = q.shape
    kv_map = lambda b, h, i: (b, 0, h)
    return pl.pallas_call(
        functools.partial(_flash_kernel, tq=tq, nheads=nheads),
        out_shape=jax.ShapeDtypeStruct((bsz, seq, MLA_WIDTH), BF16),
        grid=(bsz, MLA_HEADS // nheads, seq // tq),
        in_specs=[pl.BlockSpec((1, tq, nheads * Q_SLOT), lambda b, h, i: (b, i, h)),
                  pl.BlockSpec((1, seq, nheads * QK_NOPE_DIM), kv_map),
                  pl.BlockSpec((1, seq, LANE), lambda b, h, i: (b, 0, 0)),
                  pl.BlockSpec((1, seq, nheads * V_HEAD_DIM), kv_map)],
        out_specs=pl.BlockSpec((1, tq, nheads * V_HEAD_DIM), lambda b, h, i: (b, i, h)),
        compiler_params=_cparams(("parallel", "parallel", "arbitrary")),
        name="mla_flash",
    )(q, kn, kr, v)


def _out_proj_kernel(yr_ref, ym_ref, wr_ref, wm_ref, h_ref, g_ref, o_ref):
    y = _dot(yr_ref[...], wr_ref[...]) + _dot(ym_ref[...], wm_ref[...])
    o_ref[...] = h_ref[...] + _rms(y, g_ref[...])


def _out_proj(yr, ym, wr, wm, h, g, *, tm):
    t, d = h.shape
    half = yr.shape[1]
    row = lambda i: (i, 0)
    const = lambda i: (0, 0)
    return pl.pallas_call(
        _out_proj_kernel,
        out_shape=jax.ShapeDtypeStruct((t, d), F32),
        grid=(t // tm,),
        in_specs=[pl.BlockSpec((tm, half), row), pl.BlockSpec((tm, half), row),
                  pl.BlockSpec((half, d), const), pl.BlockSpec((half, d), const),
                  pl.BlockSpec((tm, d), row), pl.BlockSpec((1, d), const)],
        out_specs=pl.BlockSpec((tm, d), row),
        compiler_params=_cparams(("parallel",)),
        name="out_proj",
    )(yr, ym, wr, wm, h, g)


def _mem_attn_kernel(h_ref, gpre_ref, wq_ref, k_ref, v_ref, wo_ref, gpost_ref, o_ref):
    h = h_ref[...]
    u = _rms(h, gpre_ref[...]).astype(BF16)
    q = (_dot(u, wq_ref[...]) * (MEM_HEAD_DIM ** -0.5)).astype(BF16)
    outs = []
    for hd in range(MEM_HEADS):
        sl = slice(hd * MEM_HEAD_DIM, (hd + 1) * MEM_HEAD_DIM)
        s = _dot_nt(q[:, sl], k_ref[0][:, sl])
        p = jnp.exp(s - jnp.max(s, axis=-1, keepdims=True))
        denom = jnp.sum(p, axis=-1, keepdims=True)
        p = (p / denom).astype(BF16)
        outs.append(_dot(p, v_ref[0][:, sl]))
    o = jnp.concatenate(outs, axis=1).astype(BF16)
    y = _dot(o, wo_ref[...])
    o_ref[...] = h + _rms(y, gpost_ref[...])


def _mem_attn(h, gpre, wq, kmem, vmem, wo, gpost, *, seq, tm):
    t, d = h.shape
    nmem, mw = kmem.shape[1], kmem.shape[2]
    sb = seq // tm
    row = lambda i: (i, 0)
    const = lambda i: (0, 0)
    bat = lambda i: (i // sb, 0, 0)
    return pl.pallas_call(
        _mem_attn_kernel,
        out_shape=jax.ShapeDtypeStruct((t, d), F32),
        grid=(t // tm,),
        in_specs=[pl.BlockSpec((tm, d), row), pl.BlockSpec((1, d), const), pl.BlockSpec((d, mw), const),
                  pl.BlockSpec((1, nmem, mw), bat), pl.BlockSpec((1, nmem, mw), bat),
                  pl.BlockSpec((mw, d), const), pl.BlockSpec((1, d), const)],
        out_specs=pl.BlockSpec((tm, d), row),
        compiler_params=_cparams(("parallel",)),
        name="mem_attn",
    )(h, gpre, wq, kmem, vmem, wo, gpost)


def _ffn_kernel(h_ref, gpre_ref, w1_ref, w2_ref, gpost_ref, o_ref, u_ref, acc_ref):
    j = pl.program_id(1)

    @pl.when(j == 0)
    def _():
        u_ref[...] = _rms(h_ref[...], gpre_ref[...]).astype(BF16)
        acc_ref[...] = jnp.zeros_like(acc_ref)

    a = jnp.maximum(_dot(u_ref[...], w1_ref[...]), 0.0)
    acc_ref[...] += _dot((a * a).astype(BF16), w2_ref[...])

    @pl.when(j == pl.num_programs(1) - 1)
    def _():
        o_ref[...] = h_ref[...] + _rms(acc_ref[...], gpost_ref[...])


def _ffn(h, gpre, w1, w2, gpost, *, tm, tf):
    t, d = h.shape
    dff = w1.shape[1]
    row = lambda i, j: (i, 0)
    const = lambda i, j: (0, 0)
    return pl.pallas_call(
        _ffn_kernel,
        out_shape=jax.ShapeDtypeStruct((t, d), F32),
        grid=(t // tm, dff // tf),
        in_specs=[pl.BlockSpec((tm, d), row), pl.BlockSpec((1, d), const),
                  pl.BlockSpec((d, tf), lambda i, j: (0, j)), pl.BlockSpec((tf, d), lambda i, j: (j, 0)),
                  pl.BlockSpec((1, d), const)],
        out_specs=pl.BlockSpec((tm, d), row),
        scratch_shapes=[pltpu.VMEM((tm, d), BF16), pltpu.VMEM((tm, d), F32)],
        compiler_params=_cparams(("parallel", "arbitrary")),
        name="ffn",
    )(h, gpre, w1, w2, gpost)


def _pad_cols(w, n):
    return jnp.pad(w, ((0, 0), (0, n - w.shape[1])))


def _pad_rows(w, n):
    return jnp.pad(w, ((0, n - w.shape[0]), (0, 0)))


def _pad_vec(v, n):
    return jnp.pad(v, (0, n - v.shape[0])).reshape(1, n)


def _layer_weights(l, w_in, w_in_vres, mu_rwkv, mu_vres, w0, w_up, a0, a_up, v0, v_up, g_up, k_k, k_a,
                   q_norm_g, w_uq, kv_norm_g, w_ukv):
    c = RWKV_WIDTH
    wi = w_in[l]
    mla0 = 3 * c + DECAY_LORA + ICLR_LORA + GATE_LORA
    half = QK_ROPE_DIM // 2
    kr0 = mla0 + Q_LORA_RANK + KV_LORA_RANK
    kr1, kr2 = wi[:, kr0:kr0 + half], wi[:, kr0 + half:kr0 + QK_ROPE_DIM]
    vres = w_in_vres[l - 1] if l > 0 else jnp.zeros((wi.shape[0], VRES_LORA), wi.dtype)
    w_all = jnp.concatenate([
        wi[:, :3 * c],
        _pad_cols(wi[:, 3 * c:3 * c + DECAY_LORA], LANE),
        _pad_cols(wi[:, 3 * c + DECAY_LORA:3 * c + DECAY_LORA + ICLR_LORA], LANE),
        wi[:, 3 * c + DECAY_LORA + ICLR_LORA:mla0],
        wi[:, mla0:kr0],
        kr1, kr2, kr2, kr1,
        _pad_cols(vres, LANE)], axis=1).astype(BF16)
    mu = mu_rwkv[l]
    mu_all = jnp.concatenate([
        mu[:3 * c], jnp.pad(mu[3 * c:3 * c + DECAY_LORA], (0, LANE - DECAY_LORA)),
        jnp.pad(mu[3 * c + DECAY_LORA:3 * c + DECAY_LORA + ICLR_LORA], (0, LANE - ICLR_LORA)),
        mu[3 * c + DECAY_LORA + ICLR_LORA:]]).reshape(1, RWKV_COLS)

    qd = QK_NOPE_DIM + QK_ROPE_DIM
    wq = w_uq[l].reshape(Q_LORA_RANK, MLA_HEADS, qd)
    qn, q1, q2 = wq[..., :QK_NOPE_DIM], wq[..., QK_NOPE_DIM:QK_NOPE_DIM + half], wq[..., QK_NOPE_DIM + half:]
    zpad = jnp.zeros((Q_LORA_RANK, MLA_HEADS, Q_SLOT - qd), wq.dtype)
    w_qa = jnp.concatenate([qn, q1, q2, zpad], axis=-1).reshape(Q_LORA_RANK, -1).astype(BF16)
    w_qb = jnp.concatenate([jnp.zeros_like(qn), q2, q1, zpad], axis=-1).reshape(Q_LORA_RANK, -1).astype(BF16)
    wkv = w_ukv[l].reshape(KV_LORA_RANK, MLA_HEADS, QK_NOPE_DIM + V_HEAD_DIM)
    lw = {
        "w_all": w_all, "mu": mu_all,
        "w0": w0[l].reshape(1, c), "a0": a0[l].reshape(1, c),
        "w_up": _pad_rows(w_up[l], LANE).astype(BF16), "a_up": _pad_rows(a_up[l], LANE).astype(BF16),
        "g_up": g_up[l].astype(BF16), "k_k": k_k[l].reshape(1, c), "k_a": k_a[l].reshape(1, c),
        "q_norm_g": q_norm_g[l].reshape(1, -1), "kv_norm_g": kv_norm_g[l].reshape(1, -1),
        "w_qa": w_qa, "w_qb": w_qb,
        "w_kn": wkv[..., :QK_NOPE_DIM].reshape(KV_LORA_RANK, -1).astype(BF16),
        "w_v": wkv[..., QK_NOPE_DIM:].reshape(KV_LORA_RANK, -1).astype(BF16),
    }
    if l > 0:
        lw["mu_vres"] = _pad_vec(mu_vres[l - 1], LANE)
        lw["v0"] = v0[l - 1].reshape(1, c)
        lw["v_up"] = _pad_rows(v_up[l - 1], LANE).astype(BF16)
    return lw


def _rope_tabs(seq):
    pos = jnp.arange(seq, dtype=F32)
    inv_freq = ROPE_THETA ** (-jnp.arange(0, QK_ROPE_DIM, 2, dtype=F32) / QK_ROPE_DIM)
    ang = pos[:, None] * inv_freq[None, :]
    cos, sin = jnp.cos(ang), jnp.sin(ang)
    one = jnp.ones((seq, QK_NOPE_DIM), F32)
    zero_n = jnp.zeros((seq, QK_NOPE_DIM), F32)
    zero_p = jnp.zeros((seq, Q_SLOT - QK_NOPE_DIM - QK_ROPE_DIM), F32)
    return {
        "ct": jnp.concatenate([one, cos, cos, zero_p], axis=1),
        "st": jnp.concatenate([zero_n, -sin, sin, zero_p], axis=1),
        "kt": jnp.concatenate([cos, cos, -sin, sin], axis=1),
    }


def _tile(n, pref):
    t = min(n, pref)
    assert n % t == 0, (n, t)
    return t


def kernel(x, mem, mem_norm_g, mix_pre_g, w_in, w_in_vres, mu_rwkv, mu_vres, w0, w_up, a0, a_up, v0, v_up, g_up, k_k, k_a, r_k, lnx_g, lnx_b, q_norm_g, w_uq, kv_norm_g, w_ukv, w_out, mix_post_g, mem_pre_g, wq_mem, wk_mem, wv_mem, wo_mem, mem_post_g, ffn_pre_g, w_ff1, w_ff2, ffn_post_g):
    bsz, seq, d = x.shape
    depth = w_in.shape[0]
    t = bsz * seq
    nmem = mem.shape[1]
    mw = wq_mem.shape[2]
    assert seq % SCAN_CHUNK == 0 and d % LANE == 0
    tm_proj = _tile(seq, 512)
    tm_prep = _tile(seq, 256)
    tm_row = _tile(seq, 512)
    tq = _tile(seq, 512)

    tabs = _rope_tabs(seq)
    w_kv_mem = jnp.concatenate([jnp.concatenate([wk_mem[l], wv_mem[l]], axis=1) for l in range(depth)],
                               axis=1).astype(BF16)
    kv_mem = _norm_matmul(mem.reshape(bsz * nmem, d), mem_norm_g, w_kv_mem,
                          tm=_tile(bsz * nmem, 256), tn=_tile(w_kv_mem.shape[1], 512))
    kv_mem = kv_mem.astype(BF16).reshape(bsz, nmem, depth, 2, mw)

    h = x.reshape(t, d)
    v_first = None
    for l in range(depth):
        lw = _layer_weights(l, w_in, w_in_vres, mu_rwkv, mu_vres, w0, w_up, a0, a_up, v0, v_up, g_up,
                            k_k, k_a, q_norm_g, w_uq, kv_norm_g, w_ukv)
        proj = _norm_matmul(h, mix_pre_g[l], lw["w_all"], tm=tm_proj, tn=_tile(PROJ_COLS, 1536))
        r, k, v, kk, b, ld, g = _rwkv_prep(proj, lw, v_first, seq=seq, tm=tm_prep)
        if l == 0:
            v_first = v
        to3 = lambda a: a.reshape(bsz, seq, RWKV_WIDTH)
        y_rwkv = _rwkv_scan(to3(r), to3(k), to3(v), to3(kk), to3(b), to3(ld), to3(g),
                            r_k[l].reshape(1, -1), lnx_g[l].reshape(1, -1), lnx_b[l].reshape(1, -1))
        q, kn, kr, vv = _mla_prep(proj, lw, tabs, seq=seq, tm=tm_row)
        y_mla = _flash_attention(q.reshape(bsz, seq, -1), kn.reshape(bsz, seq, -1),
                                 kr.reshape(bsz, seq, -1), vv.reshape(bsz, seq, -1), tq=tq, nheads=2)
        wo = w_out[l].astype(BF16)
        h = _out_proj(y_rwkv.reshape(t, -1), y_mla.reshape(t, -1), wo[:RWKV_WIDTH], wo[RWKV_WIDTH:],
                      h, mix_post_g[l].reshape(1, d), tm=tm_row)
        h = _mem_attn(h, mem_pre_g[l].reshape(1, d), wq_mem[l].astype(BF16), kv_mem[:, :, l, 0],
                      kv_mem[:, :, l, 1], wo_mem[l].astype(BF16), mem_post_g[l].reshape(1, d),
                      seq=seq, tm=tm_row)
        h = _ffn(h, ffn_pre_g[l].reshape(1, d), w_ff1[l].astype(BF16), w_ff2[l].astype(BF16),
                 ffn_post_g[l].reshape(1, d), tm=tm_row, tf=512)
    return h.reshape(bsz, seq, d)
```

```python
import functools

import jax
import jax.numpy as jnp
from jax import lax
from jax.experimental import pallas as pl
from jax.experimental.pallas import tpu as pltpu

F32 = jnp.float32
BF16 = jnp.bfloat16

NORM_EPS = 1e-6
RWKV_GN_EPS = 64e-5
ROPE_THETA = 10000.0
LOG2_E = 1.4426950408889634

RWKV_HEAD_DIM = 64
RWKV_WIDTH = 1024
DECAY_LORA = 96
ICLR_LORA = 96
GATE_LORA = 256
VRES_LORA = 64

MLA_HEADS = 8
QK_NOPE_DIM = 128
QK_ROPE_DIM = 64
V_HEAD_DIM = 128
Q_LORA_RANK = 512
KV_LORA_RANK = 256
MLA_WIDTH = MLA_HEADS * V_HEAD_DIM
Q_SLOT = 256

MEM_HEADS = 4
MEM_HEAD_DIM = 128

LANE = 128
SCAN_CHUNK = 64
SCAN_HEADS = 4
SCAN_TILE = SCAN_HEADS * RWKV_HEAD_DIM

OFF_R, OFF_K, OFF_V = 0, 1024, 2048
OFF_WL, OFF_AL, OFF_GL = 3072, 3200, 3328
RWKV_COLS = 3584
OFF_CQ, OFF_CKV, OFF_KR, OFF_VRES = 3584, 4096, 4352, 4480
PROJ_COLS = 4608

VMEM_LIMIT = 56 * 1024 * 1024


def _cparams(sem):
    return pltpu.CompilerParams(dimension_semantics=sem, vmem_limit_bytes=VMEM_LIMIT)


def _rms(x, g):
    ms = jnp.mean(x * x, axis=-1, keepdims=True)
    return x * lax.rsqrt(ms + NORM_EPS) * g


def _dot(a, b):
    return jnp.dot(a, b, preferred_element_type=F32)


def _dot_nt(a, b):
    return lax.dot_general(a, b, (((1,), (1,)), ((), ())), preferred_element_type=F32)


def _dot_tn(a, b):
    return lax.dot_general(a, b, (((0,), (0,)), ((), ())), preferred_element_type=F32)


def _split_dot(x, w):
    hi = x.astype(BF16)
    lo = (x - hi.astype(F32)).astype(BF16)
    return _dot(hi, w) + _dot(lo, w)


def _sigmoid(x):
    return 1.0 / (1.0 + jnp.exp(-x))


def _softplus(x):
    return jnp.maximum(x, 0.0) + jnp.log(1.0 + jnp.exp(-jnp.abs(x)))


def _group_ones(n, group):
    r = lax.broadcasted_iota(jnp.int32, (n, n), 0) // group
    c = lax.broadcasted_iota(jnp.int32, (n, n), 1) // group
    return (r == c).astype(BF16)


def _head_sum(x, ones):
    w = ones.shape[0]
    return jnp.concatenate(
        [_split_dot(x[:, c * w:(c + 1) * w], ones) for c in range(x.shape[1] // w)], axis=1)


def _norm_matmul_kernel(h_ref, g_ref, w_ref, o_ref, u_ref):
    @pl.when(pl.program_id(1) == 0)
    def _():
        u_ref[...] = _rms(h_ref[...], g_ref[...]).astype(BF16)

    o_ref[...] = _dot(u_ref[...], w_ref[...])


def _norm_matmul(h, g, w, *, tm, tn):
    t, d = h.shape
    n = w.shape[1]
    return pl.pallas_call(
        _norm_matmul_kernel,
        out_shape=jax.ShapeDtypeStruct((t, n), F32),
        grid=(t // tm, n // tn),
        in_specs=[pl.BlockSpec((tm, d), lambda i, j: (i, 0)),
                  pl.BlockSpec((1, d), lambda i, j: (0, 0)),
                  pl.BlockSpec((d, tn), lambda i, j: (0, j))],
        out_specs=pl.BlockSpec((tm, tn), lambda i, j: (i, j)),
        scratch_shapes=[pltpu.VMEM((tm, d), BF16)],
        compiler_params=_cparams(("parallel", "arbitrary")),
        name="norm_matmul",
    )(h, g.reshape(1, d), w)


def _shift_mix(y, prev8, mu, first):
    prev_row = jnp.where(first, 0.0, prev8[7:8, :])
    y_prev = pltpu.roll(y, 1, 0)
    row = lax.broadcasted_iota(jnp.int32, y.shape, 0)
    y_prev = jnp.where(row == 0, prev_row, y_prev)
    return y + (y_prev - y) * mu


def _rwkv_prep_kernel(*refs, seq_blocks, has_vres):
    if has_vres:
        (y_ref, yp_ref, mu_ref, w0_ref, wup_ref, a0_ref, aup_ref, gup_ref, kk_ref, ka_ref,
         yv_ref, yvp_ref, muv_ref, v0_ref, vup_ref, vf_ref,
         r_o, k_o, v_o, kkn_o, b_o, ld_o, g_o) = refs
    else:
        (y_ref, yp_ref, mu_ref, w0_ref, wup_ref, a0_ref, aup_ref, gup_ref, kk_ref, ka_ref,
         r_o, k_o, v_o, kkn_o, b_o, ld_o, g_o) = refs
    first = (pl.program_id(0) % seq_blocks) == 0
    c = _shift_mix(y_ref[...], yp_ref[...], mu_ref[...], first)
    r = c[:, OFF_R:OFF_R + RWKV_WIDTH]
    k = c[:, OFF_K:OFF_K + RWKV_WIDTH]
    v = c[:, OFF_V:OFF_V + RWKV_WIDTH]
    wl = c[:, OFF_WL:OFF_WL + LANE]
    al = c[:, OFF_AL:OFF_AL + LANE]
    gl = c[:, OFF_GL:OFF_GL + GATE_LORA]

    z = w0_ref[...] + _dot(jnp.tanh(wl).astype(BF16), wup_ref[...])
    log_w = -_softplus(-z) - 0.5
    ld_o[...] = -jnp.exp(log_w)
    a = _sigmoid(a0_ref[...] + _dot(al.astype(BF16), aup_ref[...]))
    g_o[...] = _dot(_sigmoid(gl).astype(BF16), gup_ref[...])
    if has_vres:
        vr = _shift_mix(yv_ref[...], yvp_ref[...], muv_ref[...], first)
        mix = _sigmoid(v0_ref[...] + _dot(vr.astype(BF16), vup_ref[...]))
        v = v + (vf_ref[...] - v) * mix
    ones = _group_ones(LANE, RWKV_HEAD_DIM)
    kk = k * kk_ref[...]
    kk = kk * lax.rsqrt(jnp.maximum(_head_sum(kk * kk, ones), 1e-24))
    r_o[...] = r
    k_o[...] = k * (1.0 + (a - 1.0) * ka_ref[...])
    v_o[...] = v
    kkn_o[...] = kk
    b_o[...] = kk * a


def _rwkv_prep(proj, lw, v_first, *, seq, tm):
    t = proj.shape[0]
    has_vres = v_first is not None
    cw = RWKV_WIDTH
    row = lambda i: (i, 0)
    prev = lambda i: (jnp.maximum(i * (tm // 8) - 1, 0), 0)
    const = lambda i: (0, 0)
    vec = pl.BlockSpec((1, cw), const)
    in_specs = [pl.BlockSpec((tm, RWKV_COLS), row), pl.BlockSpec((8, RWKV_COLS), prev),
                pl.BlockSpec((1, RWKV_COLS), const), vec, pl.BlockSpec((LANE, cw), const),
                vec, pl.BlockSpec((LANE, cw), const), pl.BlockSpec((GATE_LORA, cw), const), vec, vec]
    args = [proj, proj, lw["mu"], lw["w0"], lw["w_up"], lw["a0"], lw["a_up"], lw["g_up"],
            lw["k_k"], lw["k_a"]]
    if has_vres:
        vres_blk = OFF_VRES // LANE
        in_specs += [pl.BlockSpec((tm, LANE), lambda i: (i, vres_blk)),
                     pl.BlockSpec((8, LANE), lambda i: (jnp.maximum(i * (tm // 8) - 1, 0), vres_blk)),
                     pl.BlockSpec((1, LANE), const), vec, pl.BlockSpec((LANE, cw), const),
                     pl.BlockSpec((tm, cw), row)]
        args += [proj, proj, lw["mu_vres"], lw["v0"], lw["v_up"], v_first]
    out = jax.ShapeDtypeStruct((t, cw), F32)
    return pl.pallas_call(
        functools.partial(_rwkv_prep_kernel, seq_blocks=seq // tm, has_vres=has_vres),
        out_shape=[out] * 7,
        grid=(t // tm,),
        in_specs=in_specs,
        out_specs=[pl.BlockSpec((tm, cw), row)] * 7,
        compiler_params=_cparams(("parallel",)),
        name="rwkv_prep",
    )(*args)


def _rwkv_scan_kernel(r_ref, k_ref, v_ref, kk_ref, b_ref, ld_ref, g_ref, rk_ref, lg_ref, lb_ref,
                      y_ref, ht_ref, *, nbatch, ngroups):
    cs, n = SCAN_CHUNK, SCAN_TILE

    @pl.when(pl.program_id(1) == 0)
    def _():
        ht_ref[...] = jnp.zeros_like(ht_ref)

    ri = lax.broadcasted_iota(jnp.int32, (n, n), 0)
    ci = lax.broadcasted_iota(jnp.int32, (n, n), 1)
    same_head = (ri // RWKV_HEAD_DIM) == (ci // RWKV_HEAD_DIM)
    strict = (ri % cs) > (ci % cs)
    incl = (ri % cs) >= (ci % cs)
    eye = (ri == ci).astype(F32)
    ones_bd = same_head.astype(BF16)
    tr = lax.broadcasted_iota(jnp.int32, (cs, cs), 0)
    tc = lax.broadcasted_iota(jnp.int32, (cs, cs), 1)
    tri = (tr >= tc).astype(BF16)
    sub_masks = []
    s = 1
    while s < cs:
        sub_masks.append(((ri // (2 * s)) == (ci // (2 * s))) & ((ri % (2 * s)) >= s) & ((ci % (2 * s)) < s))
        s *= 2

    def bd(x):
        return jnp.where(same_head, jnp.concatenate([x] * SCAN_HEADS, axis=0), 0.0).astype(BF16)

    rows = range(nbatch * ngroups)
    lanes = lambda i: slice((i % ngroups) * n, (i % ngroups + 1) * n)
    get = lambda ref: [ref[i // ngroups, :, lanes(i)] for i in rows]
    r, k, v = get(r_ref), get(k_ref), get(v_ref)
    kk, b, ld = get(kk_ref), get(b_ref), get(ld_ref)
    ld_hi = [x.astype(BF16) for x in ld]
    cum = [_dot(tri, h) + _dot(tri, (x - h.astype(F32)).astype(BF16)) for x, h in zip(ld, ld_hi)]
    cum_c = [c[cs - 1:cs, :] for c in cum]
    ginv = [jnp.exp(-c) for c in cum]
    gtail = [jnp.exp(cc - c) for c, cc in zip(cum, cum_c)]
    a_bd = [bd(-kk[i] * jnp.exp(cum[i] - ld[i])) for i in rows]
    r_bd = [bd(r[i] * jnp.exp(cum[i])) for i in rows]
    b_bd = [bd(b[i] * ginv[i]) for i in rows]
    k_bd = [bd(k[i] * ginv[i]) for i in rows]
    v_bd = [bd(x) for x in v]

    sc = [_dot_nt(jnp.concatenate([a_bd[i], r_bd[i]], axis=0), jnp.concatenate([b_bd[i], k_bd[i]], axis=0))
          for i in rows]
    l_ab = [s[:n, :n] for s in sc]
    l_ak = [jnp.where(strict, s[:n, n:], 0.0).astype(BF16) for s in sc]
    m_rb = [jnp.where(incl, s[n:, :n], 0.0).astype(BF16) for s in sc]
    m_rk = [jnp.where(incl, s[n:, n:], 0.0).astype(BF16) for s in sc]

    t_inv = [eye + jnp.where(sub_masks[0], x, 0.0) for x in l_ab]
    for sm in sub_masks[1:]:
        t_b = [t.astype(BF16) for t in t_inv]
        x = [_dot(jnp.where(sm, l_ab[i], 0.0).astype(BF16), t_b[i]).astype(BF16) for i in rows]
        t_inv = [t_inv[i] + _dot(t_b[i], x[i]) for i in rows]

    lv = [_dot(l_ak[i], v_bd[i]).astype(BF16) for i in rows]
    tx = [_dot(t_inv[i].astype(BF16), jnp.concatenate([a_bd[i], lv[i]], axis=1)) for i in rows]
    ht = [ht_ref[i] for i in rows]
    ht_b = [h.astype(BF16) for h in ht]
    u = [(_dot_nt(tx[i][:, :n].astype(BF16), ht_b[i]) + tx[i][:, n:]).astype(BF16) for i in rows]
    o_bd = [_dot_nt(r_bd[i], ht_b[i]) + _dot(m_rb[i], u[i]) + _dot(m_rk[i], v_bd[i]) for i in rows]
    for i in rows:
        ht_ref[i] = (ht[i] * jnp.exp(cum_c[i]) + _dot_tn(u[i], bd(b[i] * gtail[i]))
                     + _dot_tn(v_bd[i], bd(k[i] * gtail[i])))

    inv_n = 1.0 / RWKV_HEAD_DIM
    for i in rows:
        ob = o_bd[i]
        o = ob[0:cs] + ob[cs:2 * cs] + ob[2 * cs:3 * cs] + ob[3 * cs:4 * cs]
        mean = _split_dot(o, ones_bd) * inv_n
        d = o - mean
        var = _split_dot(d * d, ones_bd) * inv_n
        on = d * lax.rsqrt(var + RWKV_GN_EPS) * lg_ref[:, lanes(i)] + lb_ref[:, lanes(i)]
        bonus = _split_dot(r[i] * k[i] * rk_ref[:, lanes(i)], ones_bd) * v[i]
        y_ref[i // ngroups, :, lanes(i)] = ((on + bonus) * g_ref[i // ngroups, :, lanes(i)]).astype(y_ref.dtype)


def _rwkv_scan(r, k, v, kk, b, ld, g, r_k, lnx_g, lnx_b, *, ngroups):
    bsz, seq, cw = r.shape
    width = ngroups * SCAN_TILE
    blk = pl.BlockSpec((bsz, SCAN_CHUNK, width), lambda h, c: (0, c, h))
    vec = pl.BlockSpec((1, width), lambda h, c: (0, h))
    return pl.pallas_call(
        functools.partial(_rwkv_scan_kernel, nbatch=bsz, ngroups=ngroups),
        out_shape=jax.ShapeDtypeStruct((bsz, seq, cw), BF16),
        grid=(cw // width, seq // SCAN_CHUNK),
        in_specs=[blk] * 7 + [vec] * 3,
        out_specs=blk,
        scratch_shapes=[pltpu.VMEM((bsz * ngroups, SCAN_TILE, SCAN_TILE), F32)],
        compiler_params=_cparams(("parallel", "arbitrary")),
        name="rwkv_scan",
    )(r, k, v, kk, b, ld, g, r_k, lnx_g, lnx_b)


def _mla_prep_kernel(cq_ref, ckv_ref, kr_ref, gq_ref, gkv_ref, wa_ref, wb_ref, wkn_ref, wv_ref,
                     ct_ref, st_ref, kt_ref, q_o, kn_o, kr_o, v_o, *, scale):
    cqn = _rms(cq_ref[...], gq_ref[...]).astype(BF16)
    qa = _dot(cqn, wa_ref[...])
    qb = _dot(cqn, wb_ref[...])
    ct = ct_ref[...] * scale
    st = st_ref[...] * scale
    for h in range(MLA_HEADS):
        sl = slice(h * Q_SLOT, (h + 1) * Q_SLOT)
        q_o[:, sl] = (qa[:, sl] * ct + qb[:, sl] * st).astype(BF16)
    ckvn = _rms(ckv_ref[...], gkv_ref[...]).astype(BF16)
    kn_o[...] = _dot(ckvn, wkn_ref[...]).astype(BF16)
    v_o[...] = _dot(ckvn, wv_ref[...]).astype(BF16)
    kr = kr_ref[...] * kt_ref[...]
    kr = kr + pltpu.roll(kr, QK_ROPE_DIM, 1)
    lane = lax.broadcasted_iota(jnp.int32, kr.shape, 1)
    kr_o[...] = jnp.where(lane < QK_ROPE_DIM, kr, 0.0).astype(BF16)


def _mla_prep(proj, lw, tabs, *, seq, tm):
    t = proj.shape[0]
    sb = seq // tm
    const = lambda i: (0, 0)
    pos = lambda i: (i % sb, 0)
    qw = MLA_HEADS * Q_SLOT
    return pl.pallas_call(
        functools.partial(_mla_prep_kernel, scale=LOG2_E * (QK_NOPE_DIM + QK_ROPE_DIM) ** -0.5),
        out_shape=[jax.ShapeDtypeStruct((t, qw), BF16), jax.ShapeDtypeStruct((t, MLA_WIDTH), BF16),
                   jax.ShapeDtypeStruct((t, LANE), BF16), jax.ShapeDtypeStruct((t, MLA_WIDTH), BF16)],
        grid=(t // tm,),
        in_specs=[pl.BlockSpec((tm, Q_LORA_RANK), lambda i: (i, OFF_CQ // Q_LORA_RANK)),
                  pl.BlockSpec((tm, KV_LORA_RANK), lambda i: (i, OFF_CKV // KV_LORA_RANK)),
                  pl.BlockSpec((tm, LANE), lambda i: (i, OFF_KR // LANE)),
                  pl.BlockSpec((1, Q_LORA_RANK), const), pl.BlockSpec((1, KV_LORA_RANK), const),
                  pl.BlockSpec((Q_LORA_RANK, qw), const), pl.BlockSpec((Q_LORA_RANK, qw), const),
                  pl.BlockSpec((KV_LORA_RANK, MLA_WIDTH), const),
                  pl.BlockSpec((KV_LORA_RANK, MLA_WIDTH), const),
                  pl.BlockSpec((tm, Q_SLOT), pos), pl.BlockSpec((tm, Q_SLOT), pos),
                  pl.BlockSpec((tm, LANE), pos)],
        out_specs=[pl.BlockSpec((tm, qw), lambda i: (i, 0)), pl.BlockSpec((tm, MLA_WIDTH), lambda i: (i, 0)),
                   pl.BlockSpec((tm, LANE), lambda i: (i, 0)), pl.BlockSpec((tm, MLA_WIDTH), lambda i: (i, 0))],
        compiler_params=_cparams(("parallel",)),
        name="mla_prep",
    )(proj, proj, proj, lw["q_norm_g"], lw["kv_norm_g"], lw["w_qa"], lw["w_qb"], lw["w_kn"], lw["w_v"],
      tabs["ct"], tabs["st"], tabs["kt"])


def _flash_kernel(q_ref, kn_ref, kr_ref, v_ref, o_ref, *, tq, nheads):
    i = pl.program_id(2)
    q = [q_ref[0, :, h * Q_SLOT:(h + 1) * Q_SLOT] for h in range(nheads)]

    def block(off, tk, carry, masked):
        kr = kr_ref[0, pl.ds(off, tk), :]
        heads = range(nheads)
        st = [_dot_nt(jnp.concatenate([kn_ref[0, pl.ds(off, tk), h * LANE:(h + 1) * LANE], kr], axis=1), q[h])
              for h in heads]
        if masked:
            key = lax.broadcasted_iota(jnp.int32, (tk, tq), 0)
            qry = lax.broadcasted_iota(jnp.int32, (tk, tq), 1)
            st = [jnp.where(qry >= key, s, jnp.finfo(F32).min) for s in st]
        out = []
        for h in heads:
            m_prev, l_prev, acc = carry[h]
            m_new = jnp.maximum(m_prev, jnp.max(st[h], axis=0, keepdims=True))
            alpha = jnp.exp2(m_prev - m_new)
            p = jnp.exp2(st[h] - m_new)
            l_new = alpha * l_prev + jnp.sum(p, axis=0, keepdims=True)
            pv = _dot_tn(v_ref[0, pl.ds(off, tk), h * LANE:(h + 1) * LANE], p.astype(BF16))
            out.append((m_new, l_new, alpha * acc + pv))
        return tuple(out)

    init = tuple((jnp.full((1, tq), -jnp.inf, F32), jnp.zeros((1, tq), F32),
                  jnp.zeros((V_HEAD_DIM, tq), F32)) for _ in range(nheads))
    carry = lax.fori_loop(
        0, i // 2, lambda j, c: block(pl.multiple_of(j * (2 * tq), 2 * tq), 2 * tq, c, False), init)
    carry = lax.cond(i % 2 == 1,
                     lambda c: block(pl.multiple_of((i - 1) * tq, tq), tq, c, False),
                     lambda c: c, carry)
    carry = block(pl.multiple_of(i * tq, tq), tq, carry, True)
    for h in range(nheads):
        _, l_fin, acc = carry[h]
        o_ref[0, :, h * LANE:(h + 1) * LANE] = jnp.transpose(acc / l_fin).astype(o_ref.dtype)


def _flash_attention(q, kn, kr, v, *, tq, nheads):
    bsz, seq, _ = q.shape
    kv_map = lambda b, h, i: (b, 0, h)
    return pl.pallas_call(
        functools.partial(_flash_kernel, tq=tq, nheads=nheads),
        out_shape=jax.ShapeDtypeStruct((bsz, seq, MLA_WIDTH), BF16),
        grid=(bsz, MLA_HEADS // nheads, seq // tq),
        in_specs=[pl.BlockSpec((1, tq, nheads * Q_SLOT), lambda b, h, i: (b, i, h)),
                  pl.BlockSpec((1, seq, nheads * QK_NOPE_DIM), kv_map),
                  pl.BlockSpec((1, seq, LANE), lambda b, h, i: (b, 0, 0)),
                  pl.BlockSpec((1, seq, nheads * V_HEAD_DIM), kv_map)],
        out_specs=pl.BlockSpec((1, tq, nheads * V_HEAD_DIM), lambda b, h, i: (b, i, h)),
        compiler_params=_cparams(("parallel", "parallel", "arbitrary")),
        name="mla_flash",
    )(q, kn, kr, v)


def _out_proj_kernel(yr_ref, ym_ref, wr_ref, wm_ref, h_ref, g_ref, o_ref):
    y = _dot(yr_ref[...], wr_ref[...]) + _dot(ym_ref[...], wm_ref[...])
    o_ref[...] = h_ref[...] + _rms(y, g_ref[...])


def _out_proj(yr, ym, wr, wm, h, g, *, tm):
    t, d = h.shape
    half = yr.shape[1]
    row = lambda i: (i, 0)
    const = lambda i: (0, 0)
    return pl.pallas_call(
        _out_proj_kernel,
        out_shape=jax.ShapeDtypeStruct((t, d), F32),
        grid=(t // tm,),
        in_specs=[pl.BlockSpec((tm, half), row), pl.BlockSpec((tm, half), row),
                  pl.BlockSpec((half, d), const), pl.BlockSpec((half, d), const),
                  pl.BlockSpec((tm, d), row), pl.BlockSpec((1, d), const)],
        out_specs=pl.BlockSpec((tm, d), row),
        compiler_params=_cparams(("parallel",)),
        name="out_proj",
    )(yr, ym, wr, wm, h, g)


def _mem_attn_kernel(h_ref, gpre_ref, wq_ref, k_ref, v_ref, wo_ref, gpost_ref, o_ref):
    h = h_ref[...]
    u = _rms(h, gpre_ref[...]).astype(BF16)
    q = (_dot(u, wq_ref[...]) * (MEM_HEAD_DIM ** -0.5)).astype(BF16)
    outs = []
    for hd in range(MEM_HEADS):
        sl = slice(hd * MEM_HEAD_DIM, (hd + 1) * MEM_HEAD_DIM)
        s = _dot_nt(q[:, sl], k_ref[0][:, sl])
        p = jnp.exp(s - jnp.max(s, axis=-1, keepdims=True))
        denom = jnp.sum(p, axis=-1, keepdims=True)
        p = (p / denom).astype(BF16)
        outs.append(_dot(p, v_ref[0][:, sl]))
    o = jnp.concatenate(outs, axis=1).astype(BF16)
    y = _dot(o, wo_ref[...])
    o_ref[...] = h + _rms(y, gpost_ref[...])


def _mem_attn(h, gpre, wq, kmem, vmem, wo, gpost, *, seq, tm):
    t, d = h.shape
    nmem, mw = kmem.shape[1], kmem.shape[2]
    sb = seq // tm
    row = lambda i: (i, 0)
    const = lambda i: (0, 0)
    bat = lambda i: (i // sb, 0, 0)
    return pl.pallas_call(
        _mem_attn_kernel,
        out_shape=jax.ShapeDtypeStruct((t, d), F32),
        grid=(t // tm,),
        in_specs=[pl.BlockSpec((tm, d), row), pl.BlockSpec((1, d), const), pl.BlockSpec((d, mw), const),
                  pl.BlockSpec((1, nmem, mw), bat), pl.BlockSpec((1, nmem, mw), bat),
                  pl.BlockSpec((mw, d), const), pl.BlockSpec((1, d), const)],
        out_specs=pl.BlockSpec((tm, d), row),
        compiler_params=_cparams(("parallel",)),
        name="mem_attn",
    )(h, gpre, wq, kmem, vmem, wo, gpost)


def _ffn_kernel(h_ref, gpre_ref, w1_ref, w2_ref, gpost_ref, o_ref, u_ref, acc_ref):
    j = pl.program_id(1)

    @pl.when(j == 0)
    def _():
        u_ref[...] = _rms(h_ref[...], gpre_ref[...]).astype(BF16)
        acc_ref[...] = jnp.zeros_like(acc_ref)

    a = jnp.maximum(_dot(u_ref[...], w1_ref[...]), 0.0)
    acc_ref[...] += _dot((a * a).astype(BF16), w2_ref[...])

    @pl.when(j == pl.num_programs(1) - 1)
    def _():
        o_ref[...] = h_ref[...] + _rms(acc_ref[...], gpost_ref[...])


def _ffn(h, gpre, w1, w2, gpost, *, tm, tf):
    t, d = h.shape
    dff = w1.shape[1]
    row = lambda i, j: (i, 0)
    const = lambda i, j: (0, 0)
    return pl.pallas_call(
        _ffn_kernel,
        out_shape=jax.ShapeDtypeStruct((t, d), F32),
        grid=(t // tm, dff // tf),
        in_specs=[pl.BlockSpec((tm, d), row), pl.BlockSpec((1, d), const),
                  pl.BlockSpec((d, tf), lambda i, j: (0, j)), pl.BlockSpec((tf, d), lambda i, j: (j, 0)),
                  pl.BlockSpec((1, d), const)],
        out_specs=pl.BlockSpec((tm, d), row),
        scratch_shapes=[pltpu.VMEM((tm, d), BF16), pltpu.VMEM((tm, d), F32)],
        compiler_params=_cparams(("parallel", "arbitrary")),
        name="ffn",
    )(h, gpre, w1, w2, gpost)


def _pad_cols(w, n):
    return jnp.pad(w, ((0, 0), (0, n - w.shape[1])))


def _pad_rows(w, n):
    return jnp.pad(w, ((0, n - w.shape[0]), (0, 0)))


def _pad_vec(v, n):
    return jnp.pad(v, (0, n - v.shape[0])).reshape(1, n)


def _layer_weights(l, w_in, w_in_vres, mu_rwkv, mu_vres, w0, w_up, a0, a_up, v0, v_up, g_up, k_k, k_a,
                   q_norm_g, w_uq, kv_norm_g, w_ukv):
    c = RWKV_WIDTH
    wi = w_in[l]
    mla0 = 3 * c + DECAY_LORA + ICLR_LORA + GATE_LORA
    half = QK_ROPE_DIM // 2
    kr0 = mla0 + Q_LORA_RANK + KV_LORA_RANK
    kr1, kr2 = wi[:, kr0:kr0 + half], wi[:, kr0 + half:kr0 + QK_ROPE_DIM]
    vres = w_in_vres[l - 1] if l > 0 else jnp.zeros((wi.shape[0], VRES_LORA), wi.dtype)
    w_all = jnp.concatenate([
        wi[:, :3 * c],
        _pad_cols(wi[:, 3 * c:3 * c + DECAY_LORA], LANE),
        _pad_cols(wi[:, 3 * c + DECAY_LORA:3 * c + DECAY_LORA + ICLR_LORA], LANE),
        wi[:, 3 * c + DECAY_LORA + ICLR_LORA:mla0],
        wi[:, mla0:kr0],
        kr1, kr2, kr2, kr1,
        _pad_cols(vres, LANE)], axis=1).astype(BF16)
    mu = mu_rwkv[l]
    mu_all = jnp.concatenate([
        mu[:3 * c], jnp.pad(mu[3 * c:3 * c + DECAY_LORA], (0, LANE - DECAY_LORA)),
        jnp.pad(mu[3 * c + DECAY_LORA:3 * c + DECAY_LORA + ICLR_LORA], (0, LANE - ICLR_LORA)),
        mu[3 * c + DECAY_LORA + ICLR_LORA:]]).reshape(1, RWKV_COLS)

    qd = QK_NOPE_DIM + QK_ROPE_DIM
    wq = w_uq[l].reshape(Q_LORA_RANK, MLA_HEADS, qd)
    qn, q1, q2 = wq[..., :QK_NOPE_DIM], wq[..., QK_NOPE_DIM:QK_NOPE_DIM + half], wq[..., QK_NOPE_DIM + half:]
    zpad = jnp.zeros((Q_LORA_RANK, MLA_HEADS, Q_SLOT - qd), wq.dtype)
    w_qa = jnp.concatenate([qn, q1, q2, zpad], axis=-1).reshape(Q_LORA_RANK, -1).astype(BF16)
    w_qb = jnp.concatenate([jnp.zeros_like(qn), q2, q1, zpad], axis=-1).reshape(Q_LORA_RANK, -1).astype(BF16)
    wkv = w_ukv[l].reshape(KV_LORA_RANK, MLA_HEADS, QK_NOPE_DIM + V_HEAD_DIM)
    lw = {
        "w_all": w_all, "mu": mu_all,
        "w0": w0[l].reshape(1, c), "a0": a0[l].reshape(1, c),
        "w_up": _pad_rows(w_up[l], LANE).astype(BF16), "a_up": _pad_rows(a_up[l], LANE).astype(BF16),
        "g_up": g_up[l].astype(BF16), "k_k": k_k[l].reshape(1, c), "k_a": k_a[l].reshape(1, c),
        "q_norm_g": q_norm_g[l].reshape(1, -1), "kv_norm_g": kv_norm_g[l].reshape(1, -1),
        "w_qa": w_qa, "w_qb": w_qb,
        "w_kn": wkv[..., :QK_NOPE_DIM].reshape(KV_LORA_RANK, -1).astype(BF16),
        "w_v": wkv[..., QK_NOPE_DIM:].reshape(KV_LORA_RANK, -1).astype(BF16),
    }
    if l > 0:
        lw["mu_vres"] = _pad_vec(mu_vres[l - 1], LANE)
        lw["v0"] = v0[l - 1].reshape(1, c)
        lw["v_up"] = _pad_rows(v_up[l - 1], LANE).astype(BF16)
    return lw


def _rope_tabs(seq):
    pos = jnp.arange(seq, dtype=F32)
    inv_freq = ROPE_THETA ** (-jnp.arange(0, QK_ROPE_DIM, 2, dtype=F32) / QK_ROPE_DIM)
    ang = pos[:, None] * inv_freq[None, :]
    cos, sin = jnp.cos(ang), jnp.sin(ang)
    one = jnp.ones((seq, QK_NOPE_DIM), F32)
    zero_n = jnp.zeros((seq, QK_NOPE_DIM), F32)
    zero_p = jnp.zeros((seq, Q_SLOT - QK_NOPE_DIM - QK_ROPE_DIM), F32)
    return {
        "ct": jnp.concatenate([one, cos, cos, zero_p], axis=1),
        "st": jnp.concatenate([zero_n, -sin, sin, zero_p], axis=1),
        "kt": jnp.concatenate([cos, cos, -sin, sin], axis=1),
    }


def _tile(n, pref):
    t = min(n, pref)
    assert n % t == 0, (n, t)
    return t


def kernel(x, mem, mem_norm_g, mix_pre_g, w_in, w_in_vres, mu_rwkv, mu_vres, w0, w_up, a0, a_up, v0, v_up, g_up, k_k, k_a, r_k, lnx_g, lnx_b, q_norm_g, w_uq, kv_norm_g, w_ukv, w_out, mix_post_g, mem_pre_g, wq_mem, wk_mem, wv_mem, wo_mem, mem_post_g, ffn_pre_g, w_ff1, w_ff2, ffn_post_g):
    bsz, seq, d = x.shape
    depth = w_in.shape[0]
    t = bsz * seq
    nmem = mem.shape[1]
    mw = wq_mem.shape[2]
    assert seq % SCAN_CHUNK == 0 and d % LANE == 0
    tm_proj = _tile(seq, 512)
    tm_prep = _tile(seq, 256)
    tm_row = _tile(seq, 512)
    tq = _tile(seq, 512)

    tabs = _rope_tabs(seq)
    w_kv_mem = jnp.concatenate([jnp.concatenate([wk_mem[l], wv_mem[l]], axis=1) for l in range(depth)],
                               axis=1).astype(BF16)
    kv_mem = _norm_matmul(mem.reshape(bsz * nmem, d), mem_norm_g, w_kv_mem,
                          tm=_tile(bsz * nmem, 256), tn=_tile(w_kv_mem.shape[1], 512))
    kv_mem = kv_mem.astype(BF16).reshape(bsz, nmem, depth, 2, mw)

    h = x.reshape(t, d)
    v_first = None
    for l in range(depth):
        lw = _layer_weights(l, w_in, w_in_vres, mu_rwkv, mu_vres, w0, w_up, a0, a_up, v0, v_up, g_up,
                            k_k, k_a, q_norm_g, w_uq, kv_norm_g, w_ukv)
        proj = _norm_matmul(h, mix_pre_g[l], lw["w_all"], tm=tm_proj, tn=_tile(PROJ_COLS, 1536))
        r, k, v, kk, b, ld, g = _rwkv_prep(proj, lw, v_first, seq=seq, tm=tm_prep)
        if l == 0:
            v_first = v
        to3 = lambda a: a.reshape(bsz, seq, RWKV_WIDTH)
        y_rwkv = _rwkv_scan(to3(r), to3(k), to3(v), to3(kk), to3(b), to3(ld), to3(g),
                            r_k[l].reshape(1, -1), lnx_g[l].reshape(1, -1), lnx_b[l].reshape(1, -1),
                            ngroups=2)
        q, kn, kr, vv = _mla_prep(proj, lw, tabs, seq=seq, tm=tm_row)
        y_mla = _flash_attention(q.reshape(bsz, seq, -1), kn.reshape(bsz, seq, -1),
                                 kr.reshape(bsz, seq, -1), vv.reshape(bsz, seq, -1), tq=tq, nheads=2)
        wo = w_out[l].astype(BF16)
        h = _out_proj(y_rwkv.reshape(t, -1), y_mla.reshape(t, -1), wo[:RWKV_WIDTH], wo[RWKV_WIDTH:],
                      h, mix_post_g[l].reshape(1, d), tm=tm_row)
        h = _mem_attn(h, mem_pre_g[l].reshape(1, d), wq_mem[l].astype(BF16), kv_mem[:, :, l, 0],
                      kv_mem[:, :, l, 1], wo_mem[l].astype(BF16), mem_post_g[l].reshape(1, d),
                      seq=seq, tm=tm_row)
        h = _ffn(h, ffn_pre_g[l].reshape(1, d), w_ff1[l].astype(BF16), w_ff2[l].astype(BF16),
                 ffn_post_g[l].reshape(1, d), tm=tm_row, tf=1024)
    return h.reshape(bsz, seq, d)
```

```python
import functools

import jax
import jax.numpy as jnp
from jax import lax
from jax.experimental import pallas as pl
from jax.experimental.pallas import tpu as pltpu

F32 = jnp.float32
BF16 = jnp.bfloat16

NORM_EPS = 1e-6
RWKV_GN_EPS = 64e-5
ROPE_THETA = 10000.0
LOG2_E = 1.4426950408889634

RWKV_HEAD_DIM = 64
RWKV_WIDTH = 1024
DECAY_LORA = 96
ICLR_LORA = 96
GATE_LORA = 256
VRES_LORA = 64

MLA_HEADS = 8
QK_NOPE_DIM = 128
QK_ROPE_DIM = 64
V_HEAD_DIM = 128
Q_LORA_RANK = 512
KV_LORA_RANK = 256
MLA_WIDTH = MLA_HEADS * V_HEAD_DIM
Q_SLOT = 256

MEM_HEADS = 4
MEM_HEAD_DIM = 128

LANE = 128
BF16_ROWS = 16
SCAN_CHUNK = 64
SCAN_HEADS = 4
SCAN_TILE = SCAN_HEADS * RWKV_HEAD_DIM

OFF_R, OFF_K, OFF_V = 0, 1024, 2048
OFF_WL, OFF_AL, OFF_GL = 3072, 3200, 3328
RWKV_COLS = 3584
OFF_CQ, OFF_CKV, OFF_KR, OFF_VRES = 3584, 4096, 4352, 4480
PROJ_COLS = 4608

VMEM_LIMIT = 56 * 1024 * 1024


def _cparams(sem):
    return pltpu.CompilerParams(dimension_semantics=sem, vmem_limit_bytes=VMEM_LIMIT)


def _rms(x, g):
    ms = jnp.mean(x * x, axis=-1, keepdims=True)
    return x * lax.rsqrt(ms + NORM_EPS) * g


def _dot(a, b):
    return jnp.dot(a, b, preferred_element_type=F32)


def _dot_nt(a, b):
    return lax.dot_general(a, b, (((1,), (1,)), ((), ())), preferred_element_type=F32)


def _dot_tn(a, b):
    return lax.dot_general(a, b, (((0,), (0,)), ((), ())), preferred_element_type=F32)


def _split_dot(x, w):
    hi = x.astype(BF16)
    lo = (x - hi.astype(F32)).astype(BF16)
    return _dot(hi, w) + _dot(lo, w)


def _sigmoid(x):
    return 1.0 / (1.0 + jnp.exp(-x))


def _softplus(x):
    return jnp.maximum(x, 0.0) + jnp.log(1.0 + jnp.exp(-jnp.abs(x)))


def _group_ones(n, group):
    r = lax.broadcasted_iota(jnp.int32, (n, n), 0) // group
    c = lax.broadcasted_iota(jnp.int32, (n, n), 1) // group
    return (r == c).astype(BF16)


def _head_sum(x, ones):
    w = ones.shape[0]
    return jnp.concatenate(
        [_split_dot(x[:, c * w:(c + 1) * w], ones) for c in range(x.shape[1] // w)], axis=1)


def _norm_matmul_kernel(h_ref, g_ref, w_ref, o_ref, u_ref):
    @pl.when(pl.program_id(1) == 0)
    def _():
        u_ref[...] = _rms(h_ref[...], g_ref[...]).astype(BF16)

    o_ref[...] = _dot(u_ref[...], w_ref[...])


def _norm_matmul(h, g, w, *, tm, tn):
    t, d = h.shape
    n = w.shape[1]
    return pl.pallas_call(
        _norm_matmul_kernel,
        out_shape=jax.ShapeDtypeStruct((t, n), F32),
        grid=(t // tm, n // tn),
        in_specs=[pl.BlockSpec((tm, d), lambda i, j: (i, 0)),
                  pl.BlockSpec((1, d), lambda i, j: (0, 0)),
                  pl.BlockSpec((d, tn), lambda i, j: (0, j))],
        out_specs=pl.BlockSpec((tm, tn), lambda i, j: (i, j)),
        scratch_shapes=[pltpu.VMEM((tm, d), BF16)],
        compiler_params=_cparams(("parallel", "arbitrary")),
        name="norm_matmul",
    )(h, g.reshape(1, d), w)


def _shift_mix(y, prev8, mu, first):
    prev_row = jnp.where(first, 0.0, prev8[7:8, :])
    y_prev = pltpu.roll(y, 1, 0)
    row = lax.broadcasted_iota(jnp.int32, y.shape, 0)
    y_prev = jnp.where(row == 0, prev_row, y_prev)
    return y + (y_prev - y) * mu


def _rwkv_prep_kernel(*refs, seq_blocks, has_vres):
    if has_vres:
        (y_ref, yp_ref, mu_ref, w0_ref, wup_ref, a0_ref, aup_ref, gup_ref, kk_ref, ka_ref,
         yv_ref, yvp_ref, muv_ref, v0_ref, vup_ref, vf_ref,
         r_o, k_o, v_o, kkn_o, b_o, ld_o, g_o) = refs
    else:
        (y_ref, yp_ref, mu_ref, w0_ref, wup_ref, a0_ref, aup_ref, gup_ref, kk_ref, ka_ref,
         r_o, k_o, v_o, kkn_o, b_o, ld_o, g_o) = refs
    first = (pl.program_id(0) % seq_blocks) == 0
    c = _shift_mix(y_ref[...], yp_ref[...], mu_ref[...], first)
    r = c[:, OFF_R:OFF_R + RWKV_WIDTH]
    k = c[:, OFF_K:OFF_K + RWKV_WIDTH]
    v = c[:, OFF_V:OFF_V + RWKV_WIDTH]
    wl = c[:, OFF_WL:OFF_WL + LANE]
    al = c[:, OFF_AL:OFF_AL + LANE]
    gl = c[:, OFF_GL:OFF_GL + GATE_LORA]

    z = w0_ref[...] + _dot(jnp.tanh(wl).astype(BF16), wup_ref[...])
    log_w = -_softplus(-z) - 0.5
    ld_o[...] = -jnp.exp(log_w)
    a = _sigmoid(a0_ref[...] + _dot(al.astype(BF16), aup_ref[...]))
    g_o[...] = _dot(_sigmoid(gl).astype(BF16), gup_ref[...]).astype(g_o.dtype)
    if has_vres:
        vr = _shift_mix(yv_ref[...], yvp_ref[...], muv_ref[...], first)
        mix = _sigmoid(v0_ref[...] + _dot(vr.astype(BF16), vup_ref[...]))
        v = v + (vf_ref[...] - v) * mix
    ones = _group_ones(LANE, RWKV_HEAD_DIM)
    kk = k * kk_ref[...]
    kk = kk * lax.rsqrt(jnp.maximum(_head_sum(kk * kk, ones), 1e-24))
    r_o[...] = r.astype(r_o.dtype)
    k_o[...] = (k * (1.0 + (a - 1.0) * ka_ref[...])).astype(k_o.dtype)
    v_o[...] = v
    kkn_o[...] = kk.astype(kkn_o.dtype)
    b_o[...] = (kk * a).astype(b_o.dtype)


def _rwkv_prep(proj, lw, v_first, *, seq, tm):
    t = proj.shape[0]
    has_vres = v_first is not None
    cw = RWKV_WIDTH
    row = lambda i: (i, 0)
    prev = lambda i: (jnp.maximum(i * (tm // 8) - 1, 0), 0)
    const = lambda i: (0, 0)
    vec = pl.BlockSpec((1, cw), const)
    in_specs = [pl.BlockSpec((tm, RWKV_COLS), row), pl.BlockSpec((8, RWKV_COLS), prev),
                pl.BlockSpec((1, RWKV_COLS), const), vec, pl.BlockSpec((LANE, cw), const),
                vec, pl.BlockSpec((LANE, cw), const), pl.BlockSpec((GATE_LORA, cw), const), vec, vec]
    args = [proj, proj, lw["mu"], lw["w0"], lw["w_up"], lw["a0"], lw["a_up"], lw["g_up"],
            lw["k_k"], lw["k_a"]]
    if has_vres:
        vres_blk = OFF_VRES // LANE
        in_specs += [pl.BlockSpec((tm, LANE), lambda i: (i, vres_blk)),
                     pl.BlockSpec((8, LANE), lambda i: (jnp.maximum(i * (tm // 8) - 1, 0), vres_blk)),
                     pl.BlockSpec((1, LANE), const), vec, pl.BlockSpec((LANE, cw), const),
                     pl.BlockSpec((tm, cw), row)]
        args += [proj, proj, lw["mu_vres"], lw["v0"], lw["v_up"], v_first]
    out = lambda dt: jax.ShapeDtypeStruct((t, cw), dt)
    return pl.pallas_call(
        functools.partial(_rwkv_prep_kernel, seq_blocks=seq // tm, has_vres=has_vres),
        out_shape=[out(BF16), out(BF16), out(F32), out(BF16), out(BF16), out(F32), out(BF16)],
        grid=(t // tm,),
        in_specs=in_specs,
        out_specs=[pl.BlockSpec((tm, cw), row)] * 7,
        compiler_params=_cparams(("parallel",)),
        name="rwkv_prep",
    )(*args)


def _rwkv_scan_kernel(r_ref, k_ref, v_ref, kk_ref, b_ref, ld_ref, g_ref, rk_ref, lg_ref, lb_ref,
                      y_ref, hn_ref, *, nbatch, ngroups):
    cs, n = SCAN_CHUNK, SCAN_TILE

    @pl.when(pl.program_id(1) == 0)
    def _():
        hn_ref[...] = jnp.zeros_like(hn_ref)

    ri = lax.broadcasted_iota(jnp.int32, (n, n), 0)
    ci = lax.broadcasted_iota(jnp.int32, (n, n), 1)
    same_head = (ri // RWKV_HEAD_DIM) == (ci // RWKV_HEAD_DIM)
    strict_bd = same_head & ((ri % cs) > (ci % cs))
    incl_bd = same_head & ((ri % cs) >= (ci % cs))
    eye = (ri == ci).astype(F32)
    ones_bd = same_head.astype(BF16)
    tr = lax.broadcasted_iota(jnp.int32, (cs, cs), 0)
    tc = lax.broadcasted_iota(jnp.int32, (cs, cs), 1)
    tri = (tr >= tc).astype(BF16)
    sub_masks = []
    s = 1
    while s < cs:
        sub_masks.append(((ri // (2 * s)) == (ci // (2 * s))) & ((ri % (2 * s)) >= s) & ((ci % (2 * s)) < s))
        s *= 2

    hd = RWKV_HEAD_DIM
    low = lax.broadcasted_iota(jnp.int32, (n, LANE), 1) < hd

    def bd(x):
        return jnp.where(same_head, jnp.concatenate([x.astype(BF16)] * SCAN_HEADS, axis=0), 0.0)

    def stack(x):
        xr = pltpu.roll(x, hd, 1)
        lo = jnp.concatenate([x[:, :LANE], xr[:, LANE:], x[:, LANE:], xr[:, :LANE]], axis=0)
        hi = jnp.concatenate([xr[:, :LANE], x[:, :LANE], xr[:, LANE:], x[:, LANE:]], axis=0)
        return lo, hi

    def halves(x):
        xr = pltpu.roll(x, hd, 1)
        lo_mask = low if x.shape[0] == n else lax.broadcasted_iota(jnp.int32, x.shape, 1) < hd
        a2 = jnp.where(lo_mask, x, xr)
        b2 = jnp.where(lo_mask, xr, x)
        return jnp.concatenate([a2, a2], axis=1), jnp.concatenate([b2, b2], axis=1)

    rows = range(nbatch * ngroups)
    lanes = lambda i: slice((i % ngroups) * n, (i % ngroups + 1) * n)
    get = lambda ref: [ref[i // ngroups, :, lanes(i)].astype(F32) for i in rows]
    r, k, v = get(r_ref), get(k_ref), get(v_ref)
    kk, b, ld = get(kk_ref), get(b_ref), get(ld_ref)
    ld_hi = [x.astype(BF16) for x in ld]
    cum = [_dot(tri, h) + _dot(tri, (x - h.astype(F32)).astype(BF16)) for x, h in zip(ld, ld_hi)]
    cum_c = [c[cs - 1:cs, :] for c in cum]
    ginv = [jnp.exp(-c) for c in cum]
    gtail = [jnp.exp(cc - c) for c, cc in zip(cum, cum_c)]
    at = [(-kk[i] * jnp.exp(cum[i] - ld[i])).astype(BF16) for i in rows]
    a_bd = [bd(x) for x in at]
    r_bd = [bd(r[i] * jnp.exp(cum[i])) for i in rows]
    bk = [jnp.concatenate([(b[i] * ginv[i]).astype(BF16), (k[i] * ginv[i]).astype(BF16)], axis=0) for i in rows]

    sc = [_dot_nt(jnp.concatenate([a_bd[i], r_bd[i]], axis=0), bk[i]).astype(BF16) for i in rows]
    sc_a = [halves(s[:n]) for s in sc]
    sc_r = [halves(s[n:]) for s in sc]
    l_ab = [x[0] for x in sc_a]
    l_ak = [jnp.where(strict_bd, x[1], 0.0) for x in sc_a]
    m_rbk = [jnp.concatenate([jnp.where(incl_bd, x[0], 0.0), jnp.where(incl_bd, x[1], 0.0)], axis=1) for x in sc_r]

    def lower_rows(x, s):
        return jnp.concatenate([x[j * 2 * s + s:(j + 1) * 2 * s] for j in range(n // (2 * s))], axis=0)

    def with_lower_rows(x, new, s):
        return jnp.concatenate([p for j in range(n // (2 * s))
                                for p in (x[j * 2 * s:j * 2 * s + s], new[j * s:(j + 1) * s])], axis=0)

    t_inv = [eye + jnp.where(sub_masks[0], x, 0.0).astype(F32) for x in l_ab]
    for lvl, sm in enumerate(sub_masks[1:], start=1):
        s = 2 ** lvl
        t_b = [t.astype(BF16) for t in t_inv]
        lm = [jnp.where(sm, x, 0.0) for x in l_ab]
        if s >= BF16_ROWS:
            x = [_dot(lower_rows(lm[i], s), t_b[i]).astype(BF16) for i in rows]
            x = [with_lower_rows(jnp.zeros((n, n), BF16), xi, s) for xi in x]
            new = [lower_rows(t_inv[i], s) + _dot(lower_rows(t_b[i], s), x[i]) for i in rows]
            t_inv = [with_lower_rows(t_inv[i], new[i], s) for i in rows]
        else:
            x = [_dot(lm[i], t_b[i]).astype(BF16) for i in rows]
            t_inv = [t_inv[i] + _dot(t_b[i], x[i]) for i in rows]

    v_lo, v_hi = zip(*[stack(x.astype(BF16)) for x in v])
    a_hi = [stack(x)[1] for x in at]
    lv = [_dot(l_ak[i], v_lo[i]).astype(BF16) for i in rows]
    tx = [_dot(t_inv[i].astype(BF16), jnp.where(low, lv[i], a_hi[i])) for i in rows]
    w_bd = [jnp.where(same_head, halves(x.astype(BF16))[1], 0.0) for x in tx]
    tx_dup = [jnp.where(low, x, pltpu.roll(x, hd, 1)) for x in tx]
    hn = [hn_ref[i] for i in rows]
    hn_b = [jnp.concatenate([h.astype(BF16)] * 2, axis=0) for h in hn]
    u = [(_dot_nt(w_bd[i], hn_b[i]) + tx_dup[i]).astype(BF16) for i in rows]
    uv = [jnp.concatenate([u[i], jnp.where(low, v_lo[i], v_hi[i])], axis=0) for i in rows]
    o_st = [_dot_nt(r_bd[i], hn_b[i]) + _dot(m_rbk[i], uv[i]) for i in rows]
    for i in rows:
        bkp = jnp.concatenate([bd(b[i] * gtail[i]), bd(k[i] * gtail[i])], axis=0)
        hn_ref[i] = hn[i] * jnp.exp(cum_c[i]) + _dot_tn(uv[i], bkp)[:hd]

    inv_n = 1.0 / RWKV_HEAD_DIM
    low_c = lax.broadcasted_iota(jnp.int32, (cs, LANE), 1) < hd
    for i in rows:
        ob = o_st[i]
        o = jnp.concatenate([jnp.where(low_c, ob[0:cs], ob[cs:2 * cs]),
                             jnp.where(low_c, ob[2 * cs:3 * cs], ob[3 * cs:4 * cs])], axis=1)
        mean = _split_dot(o, ones_bd) * inv_n
        d = o - mean
        var = _split_dot(d * d, ones_bd) * inv_n
        on = d * lax.rsqrt(var + RWKV_GN_EPS) * lg_ref[:, lanes(i)] + lb_ref[:, lanes(i)]
        bonus = _split_dot(r[i] * k[i] * rk_ref[:, lanes(i)], ones_bd) * v[i]
        y_ref[i // ngroups, :, lanes(i)] = ((on + bonus) * g_ref[i // ngroups, :, lanes(i)].astype(F32)).astype(y_ref.dtype)


def _rwkv_scan(r, k, v, kk, b, ld, g, r_k, lnx_g, lnx_b, *, ngroups):
    bsz, seq, cw = r.shape
    width = ngroups * SCAN_TILE
    blk = pl.BlockSpec((bsz, SCAN_CHUNK, width), lambda h, c: (0, c, h))
    vec = pl.BlockSpec((1, width), lambda h, c: (0, h))
    return pl.pallas_call(
        functools.partial(_rwkv_scan_kernel, nbatch=bsz, ngroups=ngroups),
        out_shape=jax.ShapeDtypeStruct((bsz, seq, cw), BF16),
        grid=(cw // width, seq // SCAN_CHUNK),
        in_specs=[blk] * 7 + [vec] * 3,
        out_specs=blk,
        scratch_shapes=[pltpu.VMEM((bsz * ngroups, RWKV_HEAD_DIM, SCAN_TILE), F32)],
        compiler_params=_cparams(("parallel", "arbitrary")),
        name="rwkv_scan",
    )(r, k, v, kk, b, ld, g, r_k, lnx_g, lnx_b)


def _mla_prep_kernel(cq_ref, ckv_ref, kr_ref, gq_ref, gkv_ref, wa_ref, wb_ref, wkn_ref, wv_ref,
                     ct_ref, st_ref, kt_ref, q_o, kn_o, kr_o, v_o, *, scale):
    cqn = _rms(cq_ref[...], gq_ref[...]).astype(BF16)
    qa = _dot(cqn, wa_ref[...])
    qb = _dot(cqn, wb_ref[...])
    ct = ct_ref[...] * scale
    st = st_ref[...] * scale
    for h in range(MLA_HEADS):
        sl = slice(h * Q_SLOT, (h + 1) * Q_SLOT)
        q_o[:, sl] = (qa[:, sl] * ct + qb[:, sl] * st).astype(BF16)
    ckvn = _rms(ckv_ref[...], gkv_ref[...]).astype(BF16)
    kn_o[...] = _dot(ckvn, wkn_ref[...]).astype(BF16)
    v_o[...] = _dot(ckvn, wv_ref[...]).astype(BF16)
    kr = kr_ref[...] * kt_ref[...]
    kr = kr + pltpu.roll(kr, QK_ROPE_DIM, 1)
    lane = lax.broadcasted_iota(jnp.int32, kr.shape, 1)
    kr_o[...] = jnp.where(lane < QK_ROPE_DIM, kr, 0.0).astype(BF16)


def _mla_prep(proj, lw, tabs, *, seq, tm):
    t = proj.shape[0]
    sb = seq // tm
    const = lambda i: (0, 0)
    pos = lambda i: (i % sb, 0)
    qw = MLA_HEADS * Q_SLOT
    return pl.pallas_call(
        functools.partial(_mla_prep_kernel, scale=LOG2_E * (QK_NOPE_DIM + QK_ROPE_DIM) ** -0.5),
        out_shape=[jax.ShapeDtypeStruct((t, qw), BF16), jax.ShapeDtypeStruct((t, MLA_WIDTH), BF16),
                   jax.ShapeDtypeStruct((t, LANE), BF16), jax.ShapeDtypeStruct((t, MLA_WIDTH), BF16)],
        grid=(t // tm,),
        in_specs=[pl.BlockSpec((tm, Q_LORA_RANK), lambda i: (i, OFF_CQ // Q_LORA_RANK)),
                  pl.BlockSpec((tm, KV_LORA_RANK), lambda i: (i, OFF_CKV // KV_LORA_RANK)),
                  pl.BlockSpec((tm, LANE), lambda i: (i, OFF_KR // LANE)),
                  pl.BlockSpec((1, Q_LORA_RANK), const), pl.BlockSpec((1, KV_LORA_RANK), const),
                  pl.BlockSpec((Q_LORA_RANK, qw), const), pl.BlockSpec((Q_LORA_RANK, qw), const),
                  pl.BlockSpec((KV_LORA_RANK, MLA_WIDTH), const),
                  pl.BlockSpec((KV_LORA_RANK, MLA_WIDTH), const),
                  pl.BlockSpec((tm, Q_SLOT), pos), pl.BlockSpec((tm, Q_SLOT), pos),
                  pl.BlockSpec((tm, LANE), pos)],
        out_specs=[pl.BlockSpec((tm, qw), lambda i: (i, 0)), pl.BlockSpec((tm, MLA_WIDTH), lambda i: (i, 0)),
                   pl.BlockSpec((tm, LANE), lambda i: (i, 0)), pl.BlockSpec((tm, MLA_WIDTH), lambda i: (i, 0))],
        compiler_params=_cparams(("parallel",)),
        name="mla_prep",
    )(proj, proj, proj, lw["q_norm_g"], lw["kv_norm_g"], lw["w_qa"], lw["w_qb"], lw["w_kn"], lw["w_v"],
      tabs["ct"], tabs["st"], tabs["kt"])


def _flash_kernel(q_ref, kn_ref, kr_ref, v_ref, o_ref, *, tq, nheads):
    i = pl.program_id(2)
    q = [q_ref[0, :, h * Q_SLOT:(h + 1) * Q_SLOT] for h in range(nheads)]

    def block(off, tk, carry, masked):
        kr = kr_ref[0, pl.ds(off, tk), :]
        heads = range(nheads)
        st = [_dot_nt(jnp.concatenate([kn_ref[0, pl.ds(off, tk), h * LANE:(h + 1) * LANE], kr], axis=1), q[h])
              for h in heads]
        if masked:
            key = lax.broadcasted_iota(jnp.int32, (tk, tq), 0)
            qry = lax.broadcasted_iota(jnp.int32, (tk, tq), 1)
            st = [jnp.where(qry >= key, s, jnp.finfo(F32).min) for s in st]
        out = []
        for h in heads:
            m_prev, l_prev, acc = carry[h]
            m_new = jnp.maximum(m_prev, jnp.max(st[h], axis=0, keepdims=True))
            alpha = jnp.exp2(m_prev - m_new)
            p = jnp.exp2(st[h] - m_new)
            l_new = alpha * l_prev + jnp.sum(p, axis=0, keepdims=True)
            pv = _dot_tn(v_ref[0, pl.ds(off, tk), h * LANE:(h + 1) * LANE], p.astype(BF16))
            out.append((m_new, l_new, alpha * acc + pv))
        return tuple(out)

    init = tuple((jnp.full((1, tq), -jnp.inf, F32), jnp.zeros((1, tq), F32),
                  jnp.zeros((V_HEAD_DIM, tq), F32)) for _ in range(nheads))
    carry = lax.fori_loop(
        0, i // 2, lambda j, c: block(pl.multiple_of(j * (2 * tq), 2 * tq), 2 * tq, c, False), init)
    carry = lax.cond(i % 2 == 1,
                     lambda c: block(pl.multiple_of((i - 1) * tq, tq), tq, c, False),
                     lambda c: c, carry)
    carry = block(pl.multiple_of(i * tq, tq), tq, carry, True)
    for h in range(nheads):
        _, l_fin, acc = carry[h]
        o_ref[0, :, h * LANE:(h + 1) * LANE] = jnp.transpose(acc / l_fin).astype(o_ref.dtype)


def _flash_attention(q, kn, kr, v, *, tq, nheads):
    bsz, seq, _ = q.shape
    kv_map = lambda b, h, i: (b, 0, h)
    return pl.pallas_call(
        functools.partial(_flash_kernel, tq=tq, nheads=nheads),
        out_shape=jax.ShapeDtypeStruct((bsz, seq, MLA_WIDTH), BF16),
        grid=(bsz, MLA_HEADS // nheads, seq // tq),
        in_specs=[pl.BlockSpec((1, tq, nheads * Q_SLOT), lambda b, h, i: (b, i, h)),
                  pl.BlockSpec((1, seq, nheads * QK_NOPE_DIM), kv_map),
                  pl.BlockSpec((1, seq, LANE), lambda b, h, i: (b, 0, 0)),
                  pl.BlockSpec((1, seq, nheads * V_HEAD_DIM), kv_map)],
        out_specs=pl.BlockSpec((1, tq, nheads * V_HEAD_DIM), lambda b, h, i: (b, i, h)),
        compiler_params=_cparams(("parallel", "parallel", "arbitrary")),
        name="mla_flash",
    )(q, kn, kr, v)


def _mix_mem_kernel(yr_ref, ym_ref, wr_ref, wm_ref, h_ref, gmix_ref,
                    gpre_ref, wq_ref, k_ref, v_ref, wo_ref, gpost_ref, o_ref):
    y = _dot(yr_ref[...], wr_ref[...]) + _dot(ym_ref[...], wm_ref[...])
    h = h_ref[...] + _rms(y, gmix_ref[...])
    u = _rms(h, gpre_ref[...]).astype(BF16)
    q = (_dot(u, wq_ref[...]) * (MEM_HEAD_DIM ** -0.5)).astype(BF16)
    outs = []
    for hd in range(MEM_HEADS):
        sl = slice(hd * MEM_HEAD_DIM, (hd + 1) * MEM_HEAD_DIM)
        s = _dot_nt(q[:, sl], k_ref[0][:, sl])
        p = jnp.exp(s - jnp.max(s, axis=-1, keepdims=True))
        denom = jnp.sum(p, axis=-1, keepdims=True)
        p = (p / denom).astype(BF16)
        outs.append(_dot(p, v_ref[0][:, sl]))
    o = jnp.concatenate(outs, axis=1).astype(BF16)
    y = _dot(o, wo_ref[...])
    o_ref[...] = h + _rms(y, gpost_ref[...])


def _mix_mem(yr, ym, wr, wm, h, gmix, gpre, wq, kmem, vmem, wo, gpost, *, seq, tm):
    t, d = h.shape
    half = yr.shape[1]
    nmem, mw = kmem.shape[1], kmem.shape[2]
    sb = seq // tm
    row = lambda i: (i, 0)
    const = lambda i: (0, 0)
    bat = lambda i: (i // sb, 0, 0)
    vec = pl.BlockSpec((1, d), const)
    return pl.pallas_call(
        _mix_mem_kernel,
        out_shape=jax.ShapeDtypeStruct((t, d), F32),
        grid=(t // tm,),
        in_specs=[pl.BlockSpec((tm, half), row), pl.BlockSpec((tm, half), row),
                  pl.BlockSpec((half, d), const), pl.BlockSpec((half, d), const),
                  pl.BlockSpec((tm, d), row), vec,
                  vec, pl.BlockSpec((d, mw), const),
                  pl.BlockSpec((1, nmem, mw), bat), pl.BlockSpec((1, nmem, mw), bat),
                  pl.BlockSpec((mw, d), const), vec],
        out_specs=pl.BlockSpec((tm, d), row),
        compiler_params=_cparams(("parallel",)),
        name="mix_mem",
    )(yr, ym, wr, wm, h, gmix, gpre, wq, kmem, vmem, wo, gpost)


def _ffn_kernel(h_ref, gpre_ref, w1_ref, w2_ref, gpost_ref, o_ref, u_ref, acc_ref):
    j = pl.program_id(1)

    @pl.when(j == 0)
    def _():
        u_ref[...] = _rms(h_ref[...], gpre_ref[...]).astype(BF16)
        acc_ref[...] = jnp.zeros_like(acc_ref)

    a = jnp.maximum(_dot(u_ref[...], w1_ref[...]), 0.0)
    acc_ref[...] += _dot((a * a).astype(BF16), w2_ref[...])

    @pl.when(j == pl.num_programs(1) - 1)
    def _():
        o_ref[...] = h_ref[...] + _rms(acc_ref[...], gpost_ref[...])


def _ffn(h, gpre, w1, w2, gpost, *, tm, tf):
    t, d = h.shape
    dff = w1.shape[1]
    row = lambda i, j: (i, 0)
    const = lambda i, j: (0, 0)
    return pl.pallas_call(
        _ffn_kernel,
        out_shape=jax.ShapeDtypeStruct((t, d), F32),
        grid=(t // tm, dff // tf),
        in_specs=[pl.BlockSpec((tm, d), row), pl.BlockSpec((1, d), const),
                  pl.BlockSpec((d, tf), lambda i, j: (0, j)), pl.BlockSpec((tf, d), lambda i, j: (j, 0)),
                  pl.BlockSpec((1, d), const)],
        out_specs=pl.BlockSpec((tm, d), row),
        scratch_shapes=[pltpu.VMEM((tm, d), BF16), pltpu.VMEM((tm, d), F32)],
        compiler_params=_cparams(("parallel", "arbitrary")),
        name="ffn",
    )(h, gpre, w1, w2, gpost)


def _pad_cols(w, n):
    return jnp.pad(w, ((0, 0), (0, n - w.shape[1])))


def _pad_rows(w, n):
    return jnp.pad(w, ((0, n - w.shape[0]), (0, 0)))


def _pad_vec(v, n):
    return jnp.pad(v, (0, n - v.shape[0])).reshape(1, n)


def _layer_weights(l, w_in, w_in_vres, mu_rwkv, mu_vres, w0, w_up, a0, a_up, v0, v_up, g_up, k_k, k_a,
                   q_norm_g, w_uq, kv_norm_g, w_ukv):
    c = RWKV_WIDTH
    wi = w_in[l]
    mla0 = 3 * c + DECAY_LORA + ICLR_LORA + GATE_LORA
    half = QK_ROPE_DIM // 2
    kr0 = mla0 + Q_LORA_RANK + KV_LORA_RANK
    kr1, kr2 = wi[:, kr0:kr0 + half], wi[:, kr0 + half:kr0 + QK_ROPE_DIM]
    vres = w_in_vres[l - 1] if l > 0 else jnp.zeros((wi.shape[0], VRES_LORA), wi.dtype)
    w_all = jnp.concatenate([
        wi[:, :3 * c],
        _pad_cols(wi[:, 3 * c:3 * c + DECAY_LORA], LANE),
        _pad_cols(wi[:, 3 * c + DECAY_LORA:3 * c + DECAY_LORA + ICLR_LORA], LANE),
        wi[:, 3 * c + DECAY_LORA + ICLR_LORA:mla0],
        wi[:, mla0:kr0],
        kr1, kr2, kr2, kr1,
        _pad_cols(vres, LANE)], axis=1).astype(BF16)
    mu = mu_rwkv[l]
    mu_all = jnp.concatenate([
        mu[:3 * c], jnp.pad(mu[3 * c:3 * c + DECAY_LORA], (0, LANE - DECAY_LORA)),
        jnp.pad(mu[3 * c + DECAY_LORA:3 * c + DECAY_LORA + ICLR_LORA], (0, LANE - ICLR_LORA)),
        mu[3 * c + DECAY_LORA + ICLR_LORA:]]).reshape(1, RWKV_COLS)

    qd = QK_NOPE_DIM + QK_ROPE_DIM
    wq = w_uq[l].reshape(Q_LORA_RANK, MLA_HEADS, qd)
    qn, q1, q2 = wq[..., :QK_NOPE_DIM], wq[..., QK_NOPE_DIM:QK_NOPE_DIM + half], wq[..., QK_NOPE_DIM + half:]
    zpad = jnp.zeros((Q_LORA_RANK, MLA_HEADS, Q_SLOT - qd), wq.dtype)
    w_qa = jnp.concatenate([qn, q1, q2, zpad], axis=-1).reshape(Q_LORA_RANK, -1).astype(BF16)
    w_qb = jnp.concatenate([jnp.zeros_like(qn), q2, q1, zpad], axis=-1).reshape(Q_LORA_RANK, -1).astype(BF16)
    wkv = w_ukv[l].reshape(KV_LORA_RANK, MLA_HEADS, QK_NOPE_DIM + V_HEAD_DIM)
    lw = {
        "w_all": w_all, "mu": mu_all,
        "w0": w0[l].reshape(1, c), "a0": a0[l].reshape(1, c),
        "w_up": _pad_rows(w_up[l], LANE).astype(BF16), "a_up": _pad_rows(a_up[l], LANE).astype(BF16),
        "g_up": g_up[l].astype(BF16), "k_k": k_k[l].reshape(1, c), "k_a": k_a[l].reshape(1, c),
        "q_norm_g": q_norm_g[l].reshape(1, -1), "kv_norm_g": kv_norm_g[l].reshape(1, -1),
        "w_qa": w_qa, "w_qb": w_qb,
        "w_kn": wkv[..., :QK_NOPE_DIM].reshape(KV_LORA_RANK, -1).astype(BF16),
        "w_v": wkv[..., QK_NOPE_DIM:].reshape(KV_LORA_RANK, -1).astype(BF16),
    }
    if l > 0:
        lw["mu_vres"] = _pad_vec(mu_vres[l - 1], LANE)
        lw["v0"] = v0[l - 1].reshape(1, c)
        lw["v_up"] = _pad_rows(v_up[l - 1], LANE).astype(BF16)
    return lw


def _rope_tabs(seq):
    pos = jnp.arange(seq, dtype=F32)
    inv_freq = ROPE_THETA ** (-jnp.arange(0, QK_ROPE_DIM, 2, dtype=F32) / QK_ROPE_DIM)
    ang = pos[:, None] * inv_freq[None, :]
    cos, sin = jnp.cos(ang), jnp.sin(ang)
    one = jnp.ones((seq, QK_NOPE_DIM), F32)
    zero_n = jnp.zeros((seq, QK_NOPE_DIM), F32)
    zero_p = jnp.zeros((seq, Q_SLOT - QK_NOPE_DIM - QK_ROPE_DIM), F32)
    return {
        "ct": jnp.concatenate([one, cos, cos, zero_p], axis=1),
        "st": jnp.concatenate([zero_n, -sin, sin, zero_p], axis=1),
        "kt": jnp.concatenate([cos, cos, -sin, sin], axis=1),
    }


def _tile(n, pref):
    t = min(n, pref)
    assert n % t == 0, (n, t)
    return t


def kernel(x, mem, mem_norm_g, mix_pre_g, w_in, w_in_vres, mu_rwkv, mu_vres, w0, w_up, a0, a_up, v0, v_up, g_up, k_k, k_a, r_k, lnx_g, lnx_b, q_norm_g, w_uq, kv_norm_g, w_ukv, w_out, mix_post_g, mem_pre_g, wq_mem, wk_mem, wv_mem, wo_mem, mem_post_g, ffn_pre_g, w_ff1, w_ff2, ffn_post_g):
    bsz, seq, d = x.shape
    depth = w_in.shape[0]
    t = bsz * seq
    nmem = mem.shape[1]
    mw = wq_mem.shape[2]
    assert seq % SCAN_CHUNK == 0 and d % LANE == 0
    tm_proj = _tile(seq, 512)
    tm_prep = _tile(seq, 256)
    tm_row = _tile(seq, 512)
    tq = _tile(seq, 512)

    tabs = _rope_tabs(seq)
    w_kv_mem = jnp.concatenate([jnp.concatenate([wk_mem[l], wv_mem[l]], axis=1) for l in range(depth)],
                               axis=1).astype(BF16)
    kv_mem = _norm_matmul(mem.reshape(bsz * nmem, d), mem_norm_g, w_kv_mem,
                          tm=_tile(bsz * nmem, 256), tn=_tile(w_kv_mem.shape[1], 512))
    kv_mem = kv_mem.astype(BF16).reshape(bsz, nmem, depth, 2, mw)

    h = x.reshape(t, d)
    v_first = None
    for l in range(depth):
        lw = _layer_weights(l, w_in, w_in_vres, mu_rwkv, mu_vres, w0, w_up, a0, a_up, v0, v_up, g_up,
                            k_k, k_a, q_norm_g, w_uq, kv_norm_g, w_ukv)
        proj = _norm_matmul(h, mix_pre_g[l], lw["w_all"], tm=tm_proj, tn=_tile(PROJ_COLS, 1536))
        r, k, v, kk, b, ld, g = _rwkv_prep(proj, lw, v_first, seq=seq, tm=tm_prep)
        if l == 0:
            v_first = v
        to3 = lambda a: a.reshape(bsz, seq, RWKV_WIDTH)
        y_rwkv = _rwkv_scan(to3(r), to3(k), to3(v), to3(kk), to3(b), to3(ld), to3(g),
                            r_k[l].reshape(1, -1), lnx_g[l].reshape(1, -1), lnx_b[l].reshape(1, -1),
                            ngroups=2)
        q, kn, kr, vv = _mla_prep(proj, lw, tabs, seq=seq, tm=tm_row)
        y_mla = _flash_attention(q.reshape(bsz, seq, -1), kn.reshape(bsz, seq, -1),
                                 kr.reshape(bsz, seq, -1), vv.reshape(bsz, seq, -1), tq=tq, nheads=2)
        wo = w_out[l].astype(BF16)
        h = _mix_mem(y_rwkv.reshape(t, -1), y_mla.reshape(t, -1), wo[:RWKV_WIDTH], wo[RWKV_WIDTH:],
                     h, mix_post_g[l].reshape(1, d), mem_pre_g[l].reshape(1, d), wq_mem[l].astype(BF16),
                     kv_mem[:, :, l, 0], kv_mem[:, :, l, 1], wo_mem[l].astype(BF16),
                     mem_post_g[l].reshape(1, d), seq=seq, tm=tm_row)
        h = _ffn(h, ffn_pre_g[l].reshape(1, d), w_ff1[l].astype(BF16), w_ff2[l].astype(BF16),
                 ffn_post_g[l].reshape(1, d), tm=tm_row, tf=1024)
    return h.reshape(bsz, seq, d)
```

```python
import functools

import jax
import jax.numpy as jnp
from jax import lax
from jax.experimental import pallas as pl
from jax.experimental.pallas import tpu as pltpu

F32 = jnp.float32
BF16 = jnp.bfloat16

NORM_EPS = 1e-6
RWKV_GN_EPS = 64e-5
ROPE_THETA = 10000.0
LOG2_E = 1.4426950408889634

RWKV_HEAD_DIM = 64
RWKV_WIDTH = 1024
DECAY_LORA = 96
ICLR_LORA = 96
GATE_LORA = 256
VRES_LORA = 64

MLA_HEADS = 8
QK_NOPE_DIM = 128
QK_ROPE_DIM = 64
V_HEAD_DIM = 128
Q_LORA_RANK = 512
KV_LORA_RANK = 256
MLA_WIDTH = MLA_HEADS * V_HEAD_DIM
Q_SLOT = 256

MEM_HEADS = 4
MEM_HEAD_DIM = 128

LANE = 128
BF16_ROWS = 16
SCAN_CHUNK = 64
SCAN_HEADS = 4
SCAN_TILE = SCAN_HEADS * RWKV_HEAD_DIM

OFF_R, OFF_K, OFF_V = 0, 1024, 2048
OFF_WL, OFF_AL, OFF_GL = 3072, 3200, 3328
RWKV_COLS = 3584
OFF_CQ, OFF_CKV, OFF_KR, OFF_VRES = 3584, 4096, 4352, 4480
PROJ_COLS = 4608

VMEM_LIMIT = 56 * 1024 * 1024


def _cparams(sem):
    return pltpu.CompilerParams(dimension_semantics=sem, vmem_limit_bytes=VMEM_LIMIT)


def _rms(x, g):
    ms = jnp.mean(x * x, axis=-1, keepdims=True)
    return x * lax.rsqrt(ms + NORM_EPS) * g


def _dot(a, b):
    return jnp.dot(a, b, preferred_element_type=F32)


def _dot_nt(a, b):
    return lax.dot_general(a, b, (((1,), (1,)), ((), ())), preferred_element_type=F32)


def _dot_tn(a, b):
    return lax.dot_general(a, b, (((0,), (0,)), ((), ())), preferred_element_type=F32)


def _split_dot(x, w):
    hi = x.astype(BF16)
    lo = (x - hi.astype(F32)).astype(BF16)
    return _dot(hi, w) + _dot(lo, w)


def _sigmoid(x):
    return 1.0 / (1.0 + jnp.exp(-x))


def _softplus(x):
    return jnp.maximum(x, 0.0) + jnp.log(1.0 + jnp.exp(-jnp.abs(x)))


def _group_ones(n, group):
    r = lax.broadcasted_iota(jnp.int32, (n, n), 0) // group
    c = lax.broadcasted_iota(jnp.int32, (n, n), 1) // group
    return (r == c).astype(BF16)


def _head_sum(x, ones):
    w = ones.shape[0]
    return jnp.concatenate(
        [_split_dot(x[:, c * w:(c + 1) * w], ones) for c in range(x.shape[1] // w)], axis=1)


def _norm_matmul_kernel(h_ref, g_ref, w_ref, o_ref, u_ref):
    @pl.when(pl.program_id(1) == 0)
    def _():
        u_ref[...] = _rms(h_ref[...], g_ref[...]).astype(BF16)

    o_ref[...] = _dot(u_ref[...], w_ref[...])


def _norm_matmul(h, g, w, *, tm, tn):
    t, d = h.shape
    n = w.shape[1]
    return pl.pallas_call(
        _norm_matmul_kernel,
        out_shape=jax.ShapeDtypeStruct((t, n), F32),
        grid=(t // tm, n // tn),
        in_specs=[pl.BlockSpec((tm, d), lambda i, j: (i, 0)),
                  pl.BlockSpec((1, d), lambda i, j: (0, 0)),
                  pl.BlockSpec((d, tn), lambda i, j: (0, j))],
        out_specs=pl.BlockSpec((tm, tn), lambda i, j: (i, j)),
        scratch_shapes=[pltpu.VMEM((tm, d), BF16)],
        compiler_params=_cparams(("parallel", "arbitrary")),
        name="norm_matmul",
    )(h, g.reshape(1, d), w)


def _shift_mix(y, prev8, mu, first):
    prev_row = jnp.where(first, 0.0, prev8[7:8, :])
    y_prev = pltpu.roll(y, 1, 0)
    row = lax.broadcasted_iota(jnp.int32, y.shape, 0)
    y_prev = jnp.where(row == 0, prev_row, y_prev)
    return y + (y_prev - y) * mu


def _rwkv_prep_kernel(*refs, seq_blocks, has_vres):
    if has_vres:
        (y_ref, yp_ref, mu_ref, w0_ref, wup_ref, a0_ref, aup_ref, gup_ref, kk_ref, ka_ref,
         yv_ref, yvp_ref, muv_ref, v0_ref, vup_ref, vf_ref,
         r_o, k_o, v_o, kkn_o, b_o, ld_o, g_o) = refs
    else:
        (y_ref, yp_ref, mu_ref, w0_ref, wup_ref, a0_ref, aup_ref, gup_ref, kk_ref, ka_ref,
         r_o, k_o, v_o, kkn_o, b_o, ld_o, g_o) = refs
    first = (pl.program_id(0) % seq_blocks) == 0
    c = _shift_mix(y_ref[...], yp_ref[...], mu_ref[...], first)
    r = c[:, OFF_R:OFF_R + RWKV_WIDTH]
    k = c[:, OFF_K:OFF_K + RWKV_WIDTH]
    v = c[:, OFF_V:OFF_V + RWKV_WIDTH]
    wl = c[:, OFF_WL:OFF_WL + LANE]
    al = c[:, OFF_AL:OFF_AL + LANE]
    gl = c[:, OFF_GL:OFF_GL + GATE_LORA]

    z = w0_ref[...] + _dot(jnp.tanh(wl).astype(BF16), wup_ref[...])
    log_w = -_softplus(-z) - 0.5
    ld_o[...] = -jnp.exp(log_w)
    a = _sigmoid(a0_ref[...] + _dot(al.astype(BF16), aup_ref[...]))
    g_o[...] = _dot(_sigmoid(gl).astype(BF16), gup_ref[...]).astype(g_o.dtype)
    if has_vres:
        vr = _shift_mix(yv_ref[...], yvp_ref[...], muv_ref[...], first)
        mix = _sigmoid(v0_ref[...] + _dot(vr.astype(BF16), vup_ref[...]))
        v = v + (vf_ref[...] - v) * mix
    ones = _group_ones(LANE, RWKV_HEAD_DIM)
    kk = k * kk_ref[...]
    kk = kk * lax.rsqrt(jnp.maximum(_head_sum(kk * kk, ones), 1e-24))
    r_o[...] = r.astype(r_o.dtype)
    k_o[...] = (k * (1.0 + (a - 1.0) * ka_ref[...])).astype(k_o.dtype)
    v_o[...] = v
    kkn_o[...] = kk.astype(kkn_o.dtype)
    b_o[...] = (kk * a).astype(b_o.dtype)


def _rwkv_prep(proj, lw, v_first, *, seq, tm):
    t = proj.shape[0]
    has_vres = v_first is not None
    cw = RWKV_WIDTH
    row = lambda i: (i, 0)
    prev = lambda i: (jnp.maximum(i * (tm // 8) - 1, 0), 0)
    const = lambda i: (0, 0)
    vec = pl.BlockSpec((1, cw), const)
    in_specs = [pl.BlockSpec((tm, RWKV_COLS), row), pl.BlockSpec((8, RWKV_COLS), prev),
                pl.BlockSpec((1, RWKV_COLS), const), vec, pl.BlockSpec((LANE, cw), const),
                vec, pl.BlockSpec((LANE, cw), const), pl.BlockSpec((GATE_LORA, cw), const), vec, vec]
    args = [proj, proj, lw["mu"], lw["w0"], lw["w_up"], lw["a0"], lw["a_up"], lw["g_up"],
            lw["k_k"], lw["k_a"]]
    if has_vres:
        vres_blk = OFF_VRES // LANE
        in_specs += [pl.BlockSpec((tm, LANE), lambda i: (i, vres_blk)),
                     pl.BlockSpec((8, LANE), lambda i: (jnp.maximum(i * (tm // 8) - 1, 0), vres_blk)),
                     pl.BlockSpec((1, LANE), const), vec, pl.BlockSpec((LANE, cw), const),
                     pl.BlockSpec((tm, cw), row)]
        args += [proj, proj, lw["mu_vres"], lw["v0"], lw["v_up"], v_first]
    out = lambda dt: jax.ShapeDtypeStruct((t, cw), dt)
    return pl.pallas_call(
        functools.partial(_rwkv_prep_kernel, seq_blocks=seq // tm, has_vres=has_vres),
        out_shape=[out(BF16), out(BF16), out(F32), out(BF16), out(BF16), out(F32), out(BF16)],
        grid=(t // tm,),
        in_specs=in_specs,
        out_specs=[pl.BlockSpec((tm, cw), row)] * 7,
        compiler_params=_cparams(("parallel",)),
        name="rwkv_prep",
    )(*args)


def _rwkv_scan_kernel(r_ref, k_ref, v_ref, kk_ref, b_ref, ld_ref, g_ref, rk_ref, lg_ref, lb_ref,
                      y_ref, hn_ref, ep_ref, *, nbatch, ngroups):
    cs, n = SCAN_CHUNK, SCAN_TILE
    @pl.when(pl.program_id(1) == 0)
    def _():
        hn_ref[...] = jnp.zeros_like(hn_ref)
        ep_ref[...] = jnp.zeros_like(ep_ref)

    ri = lax.broadcasted_iota(jnp.int32, (n, n), 0)
    ci = lax.broadcasted_iota(jnp.int32, (n, n), 1)
    same_head = (ri // RWKV_HEAD_DIM) == (ci // RWKV_HEAD_DIM)
    strict_bd = same_head & ((ri % cs) > (ci % cs))
    incl_bd = same_head & ((ri % cs) >= (ci % cs))
    eye = (ri == ci).astype(F32)
    ones_bd = same_head.astype(BF16)
    tr = lax.broadcasted_iota(jnp.int32, (cs, cs), 0)
    tc = lax.broadcasted_iota(jnp.int32, (cs, cs), 1)
    tri = (tr >= tc).astype(BF16)
    sub_masks = []
    s = 1
    while s < cs:
        sub_masks.append(((ri // (2 * s)) == (ci // (2 * s))) & ((ri % (2 * s)) >= s) & ((ci % (2 * s)) < s))
        s *= 2

    hd = RWKV_HEAD_DIM
    low = lax.broadcasted_iota(jnp.int32, (n, LANE), 1) < hd

    def bd(x):
        return jnp.where(same_head, jnp.concatenate([x.astype(BF16)] * SCAN_HEADS, axis=0), 0.0)

    def stack(x):
        xr = pltpu.roll(x, hd, 1)
        lo = jnp.concatenate([x[:, :LANE], xr[:, LANE:], x[:, LANE:], xr[:, :LANE]], axis=0)
        hi = jnp.concatenate([xr[:, :LANE], x[:, :LANE], xr[:, LANE:], x[:, LANE:]], axis=0)
        return lo, hi

    def halves(x):
        xr = pltpu.roll(x, hd, 1)
        lo_mask = low if x.shape[0] == n else lax.broadcasted_iota(jnp.int32, x.shape, 1) < hd
        a2 = jnp.where(lo_mask, x, xr)
        b2 = jnp.where(lo_mask, xr, x)
        return jnp.concatenate([a2, a2], axis=1), jnp.concatenate([b2, b2], axis=1)

    rows = range(nbatch * ngroups)
    lanes = lambda i: slice((i % ngroups) * n, (i % ngroups + 1) * n)

    inv_n = 1.0 / RWKV_HEAD_DIM
    prev = {}

    def out_a(i):
        prev["o", i] = ep_ref[i, 0]
        prev["mean", i] = _split_dot(prev["o", i], ones_bd) * inv_n
        prev["bsum", i] = _split_dot(ep_ref[i, 1], ones_bd)

    def out_b(i):
        prev["d", i] = prev["o", i] - prev["mean", i]
        prev["var", i] = _split_dot(prev["d", i] * prev["d", i], ones_bd) * inv_n

    def out_c(i):
        on = (prev["d", i] * lax.rsqrt(prev["var", i] + RWKV_GN_EPS) * lg_ref[:, lanes(i)]
              + lb_ref[:, lanes(i)])
        bonus = prev["bsum", i] * ep_ref[i, 2]
        y_ref[i // ngroups, :, lanes(i)] = ((on + bonus) * ep_ref[i, 3]).astype(y_ref.dtype)

    out_tasks = [functools.partial(f, i) for f in (out_a, out_b, out_c) for i in rows]

    def issue_out(count):
        for _ in range(min(count, len(out_tasks))):
            out_tasks.pop(0)()

    get = lambda ref: [ref[i // ngroups, :, lanes(i)].astype(F32) for i in rows]
    r, k, v = get(r_ref), get(k_ref), get(v_ref)
    kk, b, ld = get(kk_ref), get(b_ref), get(ld_ref)
    ld_hi = [x.astype(BF16) for x in ld]
    cum = [_dot(tri, h) + _dot(tri, (x - h.astype(F32)).astype(BF16)) for x, h in zip(ld, ld_hi)]
    cum_c = [c[cs - 1:cs, :] for c in cum]
    issue_out(2)
    ginv = [jnp.exp(-c) for c in cum]
    gtail = [jnp.exp(cc - c) for c, cc in zip(cum, cum_c)]
    at = [(-kk[i] * jnp.exp(cum[i] - ld[i])).astype(BF16) for i in rows]
    a_bd = [bd(x) for x in at]
    r_bd = [bd(r[i] * jnp.exp(cum[i])) for i in rows]
    bk = [jnp.concatenate([(b[i] * ginv[i]).astype(BF16), (k[i] * ginv[i]).astype(BF16)], axis=0) for i in rows]

    sc = [_dot_nt(jnp.concatenate([a_bd[i], r_bd[i]], axis=0), bk[i]).astype(BF16) for i in rows]
    issue_out(2)
    sc_a = [halves(s[:n]) for s in sc]
    sc_r = [halves(s[n:]) for s in sc]
    l_ab = [x[0] for x in sc_a]
    l_ak = [jnp.where(strict_bd, x[1], 0.0) for x in sc_a]
    m_rbk = [jnp.concatenate([jnp.where(incl_bd, x[0], 0.0), jnp.where(incl_bd, x[1], 0.0)], axis=1) for x in sc_r]

    def lower_rows(x, s):
        return jnp.concatenate([x[j * 2 * s + s:(j + 1) * 2 * s] for j in range(n // (2 * s))], axis=0)

    def with_lower_rows(x, new, s):
        return jnp.concatenate([p for j in range(n // (2 * s))
                                for p in (x[j * 2 * s:j * 2 * s + s], new[j * s:(j + 1) * s])], axis=0)

    t_inv = [eye + jnp.where(sub_masks[0], x, 0.0).astype(F32) for x in l_ab]
    for lvl, sm in enumerate(sub_masks[1:], start=1):
        issue_out(3)
        s = 2 ** lvl
        t_b = [t.astype(BF16) for t in t_inv]
        lm = [jnp.where(sm, x, 0.0) for x in l_ab]
        if s >= BF16_ROWS:
            x = [_dot(lower_rows(lm[i], s), t_b[i]).astype(BF16) for i in rows]
            x = [with_lower_rows(jnp.zeros((n, n), BF16), xi, s) for xi in x]
            new = [lower_rows(t_inv[i], s) + _dot(lower_rows(t_b[i], s), x[i]) for i in rows]
            t_inv = [with_lower_rows(t_inv[i], new[i], s) for i in rows]
        else:
            x = [_dot(lm[i], t_b[i]).astype(BF16) for i in rows]
            t_inv = [t_inv[i] + _dot(t_b[i], x[i]) for i in rows]

    v_lo, v_hi = zip(*[stack(x.astype(BF16)) for x in v])
    a_hi = [stack(x)[1] for x in at]
    issue_out(2)
    lv = [_dot(l_ak[i], v_lo[i]).astype(BF16) for i in rows]
    tx = [_dot(t_inv[i].astype(BF16), jnp.where(low, lv[i], a_hi[i])) for i in rows]
    issue_out(2)
    w_bd = [jnp.where(same_head, halves(x.astype(BF16))[1], 0.0) for x in tx]
    tx_dup = [jnp.where(low, x, pltpu.roll(x, hd, 1)) for x in tx]
    hn = [hn_ref[i] for i in rows]
    hn_b = [jnp.concatenate([h.astype(BF16)] * 2, axis=0) for h in hn]
    u = [(_dot_nt(w_bd[i], hn_b[i]) + tx_dup[i]).astype(BF16) for i in rows]
    uv = [jnp.concatenate([u[i], jnp.where(low, v_lo[i], v_hi[i])], axis=0) for i in rows]
    o_st = [_dot_nt(r_bd[i], hn_b[i]) + _dot(m_rbk[i], uv[i]) for i in rows]
    for i in rows:
        bkp = jnp.concatenate([bd(b[i] * gtail[i]), bd(k[i] * gtail[i])], axis=0)
        hn_ref[i] = hn[i] * jnp.exp(cum_c[i]) + _dot_tn(uv[i], bkp)[:hd]

    issue_out(len(out_tasks))
    low_c = lax.broadcasted_iota(jnp.int32, (cs, LANE), 1) < hd
    for i in rows:
        ob = o_st[i]
        ep_ref[i, 0] = jnp.concatenate([jnp.where(low_c, ob[0:cs], ob[cs:2 * cs]),
                                              jnp.where(low_c, ob[2 * cs:3 * cs], ob[3 * cs:4 * cs])], axis=1)
        ep_ref[i, 1] = r[i] * k[i] * rk_ref[:, lanes(i)]
        ep_ref[i, 2] = v[i]
        ep_ref[i, 3] = g_ref[i // ngroups, :, lanes(i)].astype(F32)


def _rwkv_scan(r, k, v, kk, b, ld, g, r_k, lnx_g, lnx_b, *, ngroups):
    bsz, seq, cw = r.shape
    width = ngroups * SCAN_TILE
    nchunk = seq // SCAN_CHUNK
    blk = pl.BlockSpec((bsz, SCAN_CHUNK, width), lambda h, c: (0, jnp.minimum(c, nchunk - 1), h))
    out_blk = pl.BlockSpec((bsz, SCAN_CHUNK, width), lambda h, c: (0, jnp.maximum(c - 1, 0), h))
    vec = pl.BlockSpec((1, width), lambda h, c: (0, h))
    nchain = bsz * ngroups
    return pl.pallas_call(
        functools.partial(_rwkv_scan_kernel, nbatch=bsz, ngroups=ngroups),
        out_shape=jax.ShapeDtypeStruct((bsz, seq, cw), BF16),
        grid=(cw // width, nchunk + 1),
        in_specs=[blk] * 7 + [vec] * 3,
        out_specs=out_blk,
        scratch_shapes=[pltpu.VMEM((nchain, RWKV_HEAD_DIM, SCAN_TILE), F32),
                        pltpu.VMEM((nchain, 4, SCAN_CHUNK, SCAN_TILE), F32)],
        compiler_params=_cparams(("parallel", "arbitrary")),
        name="rwkv_scan",
    )(r, k, v, kk, b, ld, g, r_k, lnx_g, lnx_b)


def _mla_prep_kernel(cq_ref, ckv_ref, kr_ref, gq_ref, gkv_ref, wa_ref, wb_ref, wkn_ref, wv_ref,
                     ct_ref, st_ref, kt_ref, q_o, kn_o, kr_o, v_o, *, scale):
    cqn = _rms(cq_ref[...], gq_ref[...]).astype(BF16)
    qa = _dot(cqn, wa_ref[...])
    qb = _dot(cqn, wb_ref[...])
    ct = ct_ref[...] * scale
    st = st_ref[...] * scale
    for h in range(MLA_HEADS):
        sl = slice(h * Q_SLOT, (h + 1) * Q_SLOT)
        q_o[:, sl] = (qa[:, sl] * ct + qb[:, sl] * st).astype(BF16)
    ckvn = _rms(ckv_ref[...], gkv_ref[...]).astype(BF16)
    kn_o[...] = _dot(ckvn, wkn_ref[...]).astype(BF16)
    v_o[...] = _dot(ckvn, wv_ref[...]).astype(BF16)
    kr = kr_ref[...] * kt_ref[...]
    kr = kr + pltpu.roll(kr, QK_ROPE_DIM, 1)
    lane = lax.broadcasted_iota(jnp.int32, kr.shape, 1)
    kr_o[...] = jnp.where(lane < QK_ROPE_DIM, kr, 0.0).astype(BF16)


def _mla_prep(proj, lw, tabs, *, seq, tm):
    t = proj.shape[0]
    sb = seq // tm
    const = lambda i: (0, 0)
    pos = lambda i: (i % sb, 0)
    qw = MLA_HEADS * Q_SLOT
    return pl.pallas_call(
        functools.partial(_mla_prep_kernel, scale=LOG2_E * (QK_NOPE_DIM + QK_ROPE_DIM) ** -0.5),
        out_shape=[jax.ShapeDtypeStruct((t, qw), BF16), jax.ShapeDtypeStruct((t, MLA_WIDTH), BF16),
                   jax.ShapeDtypeStruct((t, LANE), BF16), jax.ShapeDtypeStruct((t, MLA_WIDTH), BF16)],
        grid=(t // tm,),
        in_specs=[pl.BlockSpec((tm, Q_LORA_RANK), lambda i: (i, OFF_CQ // Q_LORA_RANK)),
                  pl.BlockSpec((tm, KV_LORA_RANK), lambda i: (i, OFF_CKV // KV_LORA_RANK)),
                  pl.BlockSpec((tm, LANE), lambda i: (i, OFF_KR // LANE)),
                  pl.BlockSpec((1, Q_LORA_RANK), const), pl.BlockSpec((1, KV_LORA_RANK), const),
                  pl.BlockSpec((Q_LORA_RANK, qw), const), pl.BlockSpec((Q_LORA_RANK, qw), const),
                  pl.BlockSpec((KV_LORA_RANK, MLA_WIDTH), const),
                  pl.BlockSpec((KV_LORA_RANK, MLA_WIDTH), const),
                  pl.BlockSpec((tm, Q_SLOT), pos), pl.BlockSpec((tm, Q_SLOT), pos),
                  pl.BlockSpec((tm, LANE), pos)],
        out_specs=[pl.BlockSpec((tm, qw), lambda i: (i, 0)), pl.BlockSpec((tm, MLA_WIDTH), lambda i: (i, 0)),
                   pl.BlockSpec((tm, LANE), lambda i: (i, 0)), pl.BlockSpec((tm, MLA_WIDTH), lambda i: (i, 0))],
        compiler_params=_cparams(("parallel",)),
        name="mla_prep",
    )(proj, proj, proj, lw["q_norm_g"], lw["kv_norm_g"], lw["w_qa"], lw["w_qb"], lw["w_kn"], lw["w_v"],
      tabs["ct"], tabs["st"], tabs["kt"])


def _flash_kernel(q_ref, kn_ref, kr_ref, v_ref, o_ref, *, tq, nheads):
    i = pl.program_id(2)
    q = [q_ref[0, :, h * Q_SLOT:(h + 1) * Q_SLOT] for h in range(nheads)]

    def block(off, tk, carry, masked):
        heads = range(nheads)
        kr = kr_ref[0, pl.ds(off, tk), :]
        kn = [kn_ref[0, pl.ds(off, tk), h * LANE:(h + 1) * LANE] for h in heads]
        vv = [v_ref[0, pl.ds(off, tk), h * LANE:(h + 1) * LANE] for h in heads]
        st = [_dot_nt(jnp.concatenate([kn[h], kr], axis=1), q[h]) for h in heads]
        if masked:
            key = lax.broadcasted_iota(jnp.int32, (tk, tq), 0)
            qry = lax.broadcasted_iota(jnp.int32, (tk, tq), 1)
            st = [jnp.where(qry >= key, s, jnp.finfo(F32).min) for s in st]
        out = []
        for h in heads:
            m_prev, l_prev, acc = carry[h]
            m_new = jnp.maximum(m_prev, jnp.max(st[h], axis=0, keepdims=True))
            alpha = jnp.exp2(m_prev - m_new)
            p = jnp.exp2(st[h] - m_new)
            l_new = alpha * l_prev + jnp.sum(p, axis=0, keepdims=True)
            pv = _dot_tn(vv[h], p.astype(BF16))
            out.append((m_new, l_new, alpha * acc + pv))
        return tuple(out)

    init = tuple((jnp.full((1, tq), -jnp.inf, F32), jnp.zeros((1, tq), F32),
                  jnp.zeros((V_HEAD_DIM, tq), F32)) for _ in range(nheads))
    carry = lax.fori_loop(
        0, i // 2, lambda j, c: block(pl.multiple_of(j * (2 * tq), 2 * tq), 2 * tq, c, False), init)
    carry = lax.cond(i % 2 == 1,
                     lambda c: block(pl.multiple_of((i - 1) * tq, tq), tq, c, False),
                     lambda c: c, carry)
    carry = block(pl.multiple_of(i * tq, tq), tq, carry, True)
    for h in range(nheads):
        _, l_fin, acc = carry[h]
        o_ref[0, :, h * LANE:(h + 1) * LANE] = jnp.transpose(acc / l_fin).astype(o_ref.dtype)


def _flash_attention(q, kn, kr, v, *, tq, nheads):
    bsz, seq, _ = q.shape
    kv_map = lambda b, h, i: (b, 0, h)
    return pl.pallas_call(
        functools.partial(_flash_kernel, tq=tq, nheads=nheads),
        out_shape=jax.ShapeDtypeStruct((bsz, seq, MLA_WIDTH), BF16),
        grid=(bsz, MLA_HEADS // nheads, seq // tq),
        in_specs=[pl.BlockSpec((1, tq, nheads * Q_SLOT), lambda b, h, i: (b, i, h)),
                  pl.BlockSpec((1, seq, nheads * QK_NOPE_DIM), kv_map),
                  pl.BlockSpec((1, seq, LANE), lambda b, h, i: (b, 0, 0)),
                  pl.BlockSpec((1, seq, nheads * V_HEAD_DIM), kv_map)],
        out_specs=pl.BlockSpec((1, tq, nheads * V_HEAD_DIM), lambda b, h, i: (b, i, h)),
        compiler_params=_cparams(("parallel", "parallel", "arbitrary")),
        name="mla_flash",
    )(q, kn, kr, v)


def _mix_mem_kernel(yr_ref, ym_ref, wr_ref, wm_ref, h_ref, gmix_ref,
                    gpre_ref, wq_ref, k_ref, v_ref, wo_ref, gpost_ref, o_ref):
    y = _dot(yr_ref[...], wr_ref[...]) + _dot(ym_ref[...], wm_ref[...])
    h = h_ref[...] + _rms(y, gmix_ref[...])
    u = _rms(h, gpre_ref[...]).astype(BF16)
    q = (_dot(u, wq_ref[...]) * (MEM_HEAD_DIM ** -0.5)).astype(BF16)
    outs = []
    for hd in range(MEM_HEADS):
        sl = slice(hd * MEM_HEAD_DIM, (hd + 1) * MEM_HEAD_DIM)
        s = _dot_nt(q[:, sl], k_ref[0][:, sl])
        p = jnp.exp(s - jnp.max(s, axis=-1, keepdims=True))
        denom = jnp.sum(p, axis=-1, keepdims=True)
        p = (p / denom).astype(BF16)
        outs.append(_dot(p, v_ref[0][:, sl]))
    o = jnp.concatenate(outs, axis=1).astype(BF16)
    y = _dot(o, wo_ref[...])
    o_ref[...] = h + _rms(y, gpost_ref[...])


def _mix_mem(yr, ym, wr, wm, h, gmix, gpre, wq, kmem, vmem, wo, gpost, *, seq, tm):
    t, d = h.shape
    half = yr.shape[1]
    nmem, mw = kmem.shape[1], kmem.shape[2]
    sb = seq // tm
    row = lambda i: (i, 0)
    const = lambda i: (0, 0)
    bat = lambda i: (i // sb, 0, 0)
    vec = pl.BlockSpec((1, d), const)
    return pl.pallas_call(
        _mix_mem_kernel,
        out_shape=jax.ShapeDtypeStruct((t, d), F32),
        grid=(t // tm,),
        in_specs=[pl.BlockSpec((tm, half), row), pl.BlockSpec((tm, half), row),
                  pl.BlockSpec((half, d), const), pl.BlockSpec((half, d), const),
                  pl.BlockSpec((tm, d), row), vec,
                  vec, pl.BlockSpec((d, mw), const),
                  pl.BlockSpec((1, nmem, mw), bat), pl.BlockSpec((1, nmem, mw), bat),
                  pl.BlockSpec((mw, d), const), vec],
        out_specs=pl.BlockSpec((tm, d), row),
        compiler_params=_cparams(("parallel",)),
        name="mix_mem",
    )(yr, ym, wr, wm, h, gmix, gpre, wq, kmem, vmem, wo, gpost)


def _ffn_kernel(h_ref, gpre_ref, w1_ref, w2_ref, gpost_ref, o_ref, u_ref, acc_ref):
    j = pl.program_id(1)

    @pl.when(j == 0)
    def _():
        u_ref[...] = _rms(h_ref[...], gpre_ref[...]).astype(BF16)
        acc_ref[...] = jnp.zeros_like(acc_ref)

    a = jnp.maximum(_dot(u_ref[...], w1_ref[...]), 0.0)
    acc_ref[...] += _dot((a * a).astype(BF16), w2_ref[...])

    @pl.when(j == pl.num_programs(1) - 1)
    def _():
        o_ref[...] = h_ref[...] + _rms(acc_ref[...], gpost_ref[...])


def _ffn(h, gpre, w1, w2, gpost, layer, *, tm, tf):
    t, d = h.shape
    dff = w1.shape[2]
    row = lambda i, j: (i, 0)
    const = lambda i, j: (0, 0)
    return pl.pallas_call(
        _ffn_kernel,
        out_shape=jax.ShapeDtypeStruct((t, d), F32),
        grid=(t // tm, dff // tf),
        in_specs=[pl.BlockSpec((tm, d), row), pl.BlockSpec((1, d), const),
                  pl.BlockSpec((None, d, tf), lambda i, j: (layer, 0, j)),
                  pl.BlockSpec((None, tf, d), lambda i, j: (layer, j, 0)),
                  pl.BlockSpec((1, d), const)],
        out_specs=pl.BlockSpec((tm, d), row),
        scratch_shapes=[pltpu.VMEM((tm, d), BF16), pltpu.VMEM((tm, d), F32)],
        compiler_params=_cparams(("parallel", "arbitrary")),
        name="ffn",
    )(h, gpre, w1, w2, gpost)


def _pad_cols(w, n):
    return jnp.pad(w, ((0, 0), (0, n - w.shape[1])))


def _pad_rows(w, n):
    return jnp.pad(w, ((0, n - w.shape[0]), (0, 0)))


def _pad_vec(v, n):
    return jnp.pad(v, (0, n - v.shape[0])).reshape(1, n)


def _layer_weights(l, w_in, w_in_vres, mu_rwkv, mu_vres, w0, w_up, a0, a_up, v0, v_up, g_up, k_k, k_a,
                   q_norm_g, w_uq, kv_norm_g, w_ukv):
    c = RWKV_WIDTH
    wi = w_in[l].astype(BF16)
    mla0 = 3 * c + DECAY_LORA + ICLR_LORA + GATE_LORA
    half = QK_ROPE_DIM // 2
    kr0 = mla0 + Q_LORA_RANK + KV_LORA_RANK
    kr1, kr2 = wi[:, kr0:kr0 + half], wi[:, kr0 + half:kr0 + QK_ROPE_DIM]
    vres = w_in_vres[l - 1].astype(BF16) if l > 0 else jnp.zeros((wi.shape[0], VRES_LORA), BF16)
    w_all = jnp.concatenate([
        wi[:, :3 * c],
        _pad_cols(wi[:, 3 * c:3 * c + DECAY_LORA], LANE),
        _pad_cols(wi[:, 3 * c + DECAY_LORA:3 * c + DECAY_LORA + ICLR_LORA], LANE),
        wi[:, 3 * c + DECAY_LORA + ICLR_LORA:mla0],
        wi[:, mla0:kr0],
        kr1, kr2, kr2, kr1,
        _pad_cols(vres, LANE)], axis=1).astype(BF16)
    mu = mu_rwkv[l]
    mu_all = jnp.concatenate([
        mu[:3 * c], jnp.pad(mu[3 * c:3 * c + DECAY_LORA], (0, LANE - DECAY_LORA)),
        jnp.pad(mu[3 * c + DECAY_LORA:3 * c + DECAY_LORA + ICLR_LORA], (0, LANE - ICLR_LORA)),
        mu[3 * c + DECAY_LORA + ICLR_LORA:]]).reshape(1, RWKV_COLS)

    qd = QK_NOPE_DIM + QK_ROPE_DIM
    wq = w_uq[l].reshape(Q_LORA_RANK, MLA_HEADS, qd)
    qn, q1, q2 = wq[..., :QK_NOPE_DIM], wq[..., QK_NOPE_DIM:QK_NOPE_DIM + half], wq[..., QK_NOPE_DIM + half:]
    zpad = jnp.zeros((Q_LORA_RANK, MLA_HEADS, Q_SLOT - qd), wq.dtype)
    w_qa = jnp.concatenate([qn, q1, q2, zpad], axis=-1).reshape(Q_LORA_RANK, -1).astype(BF16)
    w_qb = jnp.concatenate([jnp.zeros_like(qn), q2, q1, zpad], axis=-1).reshape(Q_LORA_RANK, -1).astype(BF16)
    wkv = w_ukv[l].reshape(KV_LORA_RANK, MLA_HEADS, QK_NOPE_DIM + V_HEAD_DIM)
    lw = {
        "w_all": w_all, "mu": mu_all,
        "w0": w0[l].reshape(1, c), "a0": a0[l].reshape(1, c),
        "w_up": _pad_rows(w_up[l], LANE).astype(BF16), "a_up": _pad_rows(a_up[l], LANE).astype(BF16),
        "g_up": g_up[l].astype(BF16), "k_k": k_k[l].reshape(1, c), "k_a": k_a[l].reshape(1, c),
        "q_norm_g": q_norm_g[l].reshape(1, -1), "kv_norm_g": kv_norm_g[l].reshape(1, -1),
        "w_qa": w_qa, "w_qb": w_qb,
        "w_kn": wkv[..., :QK_NOPE_DIM].reshape(KV_LORA_RANK, -1).astype(BF16),
        "w_v": wkv[..., QK_NOPE_DIM:].reshape(KV_LORA_RANK, -1).astype(BF16),
    }
    if l > 0:
        lw["mu_vres"] = _pad_vec(mu_vres[l - 1], LANE)
        lw["v0"] = v0[l - 1].reshape(1, c)
        lw["v_up"] = _pad_rows(v_up[l - 1], LANE).astype(BF16)
    return lw


def _rope_tabs(seq):
    pos = jnp.arange(seq, dtype=F32)
    inv_freq = ROPE_THETA ** (-jnp.arange(0, QK_ROPE_DIM, 2, dtype=F32) / QK_ROPE_DIM)
    ang = pos[:, None] * inv_freq[None, :]
    cos, sin = jnp.cos(ang), jnp.sin(ang)
    one = jnp.ones((seq, QK_NOPE_DIM), F32)
    zero_n = jnp.zeros((seq, QK_NOPE_DIM), F32)
    zero_p = jnp.zeros((seq, Q_SLOT - QK_NOPE_DIM - QK_ROPE_DIM), F32)
    return {
        "ct": jnp.concatenate([one, cos, cos, zero_p], axis=1),
        "st": jnp.concatenate([zero_n, -sin, sin, zero_p], axis=1),
        "kt": jnp.concatenate([cos, cos, -sin, sin], axis=1),
    }


def _tile(n, pref):
    t = min(n, pref)
    assert n % t == 0, (n, t)
    return t


def kernel(x, mem, mem_norm_g, mix_pre_g, w_in, w_in_vres, mu_rwkv, mu_vres, w0, w_up, a0, a_up, v0, v_up, g_up, k_k, k_a, r_k, lnx_g, lnx_b, q_norm_g, w_uq, kv_norm_g, w_ukv, w_out, mix_post_g, mem_pre_g, wq_mem, wk_mem, wv_mem, wo_mem, mem_post_g, ffn_pre_g, w_ff1, w_ff2, ffn_post_g):
    bsz, seq, d = x.shape
    depth = w_in.shape[0]
    t = bsz * seq
    nmem = mem.shape[1]
    mw = wq_mem.shape[2]
    assert seq % SCAN_CHUNK == 0 and d % LANE == 0
    tm_proj = _tile(seq, 1024)
    tm_prep = _tile(seq, 256)
    tm_row = _tile(seq, 512)
    tq = _tile(seq, 512)

    tabs = _rope_tabs(seq)
    w_kv_mem = jnp.concatenate([jnp.concatenate([wk_mem[l], wv_mem[l]], axis=1) for l in range(depth)],
                               axis=1).astype(BF16)
    kv_mem = _norm_matmul(mem.reshape(bsz * nmem, d), mem_norm_g, w_kv_mem,
                          tm=_tile(bsz * nmem, 256), tn=_tile(w_kv_mem.shape[1], 512))
    kv_mem = kv_mem.astype(BF16).reshape(bsz, nmem, depth, 2, mw)

    w_ff1_b, w_ff2_b = w_ff1.astype(BF16), w_ff2.astype(BF16)
    h = x.reshape(t, d)
    v_first = None
    for l in range(depth):
        lw = _layer_weights(l, w_in, w_in_vres, mu_rwkv, mu_vres, w0, w_up, a0, a_up, v0, v_up, g_up,
                            k_k, k_a, q_norm_g, w_uq, kv_norm_g, w_ukv)
        proj = _norm_matmul(h, mix_pre_g[l], lw["w_all"], tm=tm_proj, tn=_tile(PROJ_COLS, 1536))
        r, k, v, kk, b, ld, g = _rwkv_prep(proj, lw, v_first, seq=seq, tm=tm_prep)
        if l == 0:
            v_first = v
        to3 = lambda a: a.reshape(bsz, seq, RWKV_WIDTH)
        y_rwkv = _rwkv_scan(to3(r), to3(k), to3(v), to3(kk), to3(b), to3(ld), to3(g),
                            r_k[l].reshape(1, -1), lnx_g[l].reshape(1, -1), lnx_b[l].reshape(1, -1),
                            ngroups=2)
        q, kn, kr, vv = _mla_prep(proj, lw, tabs, seq=seq, tm=tm_row)
        y_mla = _flash_attention(q.reshape(bsz, seq, -1), kn.reshape(bsz, seq, -1),
                                 kr.reshape(bsz, seq, -1), vv.reshape(bsz, seq, -1), tq=tq, nheads=2)
        wo = w_out[l].astype(BF16)
        h = _mix_mem(y_rwkv.reshape(t, -1), y_mla.reshape(t, -1), wo[:RWKV_WIDTH], wo[RWKV_WIDTH:],
                     h, mix_post_g[l].reshape(1, d), mem_pre_g[l].reshape(1, d), wq_mem[l].astype(BF16),
                     kv_mem[:, :, l, 0], kv_mem[:, :, l, 1], wo_mem[l].astype(BF16),
                     mem_post_g[l].reshape(1, d), seq=seq, tm=tm_row)
        h = _ffn(h, ffn_pre_g[l].reshape(1, d), w_ff1_b, w_ff2_b, ffn_post_g[l].reshape(1, d), l,
                 tm=tm_row, tf=1024)
    return h.reshape(bsz, seq, d)
```
